```python
import numpy as np
import jax, jax.numpy as jnp
from jax import lax

D_MODEL = 2048
BATCH = 4
SEQ = 4096
DEPTH = 1

HEAD_DIM = 128
NSA_Q_HEADS = 16
NSA_KV_GROUPS = 4
NSA_GROUP = NSA_Q_HEADS // NSA_KV_GROUPS
CMP_BLOCK = 32
CMP_STRIDE = 16
CMP_HIDDEN = 256
SEL_BLOCK = 64
SEL_TOPK = 16
WINDOW = 512
Q_BLOCK = 128
SEL_Q_CHUNK = 16
ROPE_THETA = 10000.0
LRU_WIDTH = 2048
LRU_BLOCKS = 16
LRU_BLOCK_W = LRU_WIDTH // LRU_BLOCKS
CONV_WIDTH = 4
LRU_C = 8.0
N_EXPERTS = 32
TOP_K = 4
D_FF = 2048
SWIGLU_ALPHA = 1.702
SWIGLU_LIMIT = 7.0
MOE_ROW_BLOCK = 256
NORM_EPS = 1e-6
NEG_INF = -1e30
FORCE_SCORE = 1e6

Q_W = NSA_Q_HEADS * HEAD_DIM
KV_W = NSA_KV_GROUPS * HEAD_DIM
IN_SPLITS = (Q_W, KV_W, KV_W, KV_W, KV_W, KV_W, KV_W, 3 * NSA_Q_HEADS, LRU_WIDTH, LRU_WIDTH, D_MODEL, D_MODEL)
IN_WIDTH = Q_W + 6 * KV_W + 3 * NSA_Q_HEADS + 2 * LRU_WIDTH + 2 * D_MODEL

kernel_name = "hybrid_nsa_rglru_moe_block"


def rms_norm(x, g):
    xf = x.astype(jnp.float32)
    y = xf * lax.rsqrt(jnp.mean(xf * xf, axis=-1, keepdims=True) + NORM_EPS)
    return (y * g.astype(jnp.float32)).astype(x.dtype)


def masked_softmax(s, mask):
    s = jnp.where(mask, s.astype(jnp.float32), NEG_INF)
    p = jax.nn.softmax(s, axis=-1)
    return jnp.where(mask, p, 0.0)


def rope_tables(seq, dtype):
    pos = jnp.arange(seq, dtype=jnp.float32)
    inv = ROPE_THETA ** (-jnp.arange(0, HEAD_DIM, 2, dtype=jnp.float32) / HEAD_DIM)
    ang = pos[:, None] * inv[None, :]
    ang = jnp.concatenate([ang, ang], axis=-1)
    return jnp.cos(ang).astype(dtype), jnp.sin(ang).astype(dtype)


def apply_rope(x, cos, sin):
    half = HEAD_DIM // 2
    rot = jnp.concatenate([-x[..., half:], x[..., :half]], axis=-1)
    return x * cos + rot * sin


def cmp_to_sel_matrix(n_cmp, n_sel):
    cs = np.arange(n_cmp) * CMP_STRIDE
    ss = np.arange(n_sel) * SEL_BLOCK
    ov = np.minimum(cs[:, None] + CMP_BLOCK, ss[None, :] + SEL_BLOCK) - np.maximum(cs[:, None], ss[None, :])
    return jnp.asarray(np.clip(ov, 0, None) / CMP_BLOCK, dtype=jnp.float32)


def compress_kv(k, pe, w1, w2):
    b, s, g, d = k.shape
    ch = k.reshape(b, s // CMP_STRIDE, CMP_STRIDE, g, d)
    blocks = jnp.concatenate([ch[:, :-1], ch[:, 1:]], axis=2)
    blocks = blocks + pe[None, None, :, None, :]
    blocks = jnp.swapaxes(blocks, 2, 3).reshape(b, -1, g, CMP_BLOCK * d)
    hid = jax.nn.gelu(blocks @ w1)
    return jnp.swapaxes(hid @ w2, 1, 2)


def nsa_mixer(q_raw, k_cmp, v_cmp, k_sel, v_sel, k_win, v_win, g_raw, pe_k, pe_v, ck_w1, ck_w2, cv_w1, cv_w2):
    b, s, _ = q_raw.shape
    G, R, DH = NSA_KV_GROUPS, NSA_GROUP, HEAD_DIM
    scale = HEAD_DIM ** -0.5
    q = q_raw.reshape(b, s, G, R, DH).transpose(0, 2, 3, 1, 4)
    cos, sin = rope_tables(s, q.dtype)
    q_rot = apply_rope(q, cos, sin)
    t_pos = jnp.arange(s)

    def heads(t):
        return t.reshape(b, s, G, DH)

    kc = compress_kv(heads(k_cmp), pe_k, ck_w1, ck_w2)
    vc = compress_kv(heads(v_cmp), pe_v, cv_w1, cv_w2)
    n_cmp = kc.shape[2]
    cmp_end = jnp.arange(n_cmp) * CMP_STRIDE + CMP_BLOCK - 1
    cmp_mask = cmp_end[None, :] <= t_pos[:, None]
    s_cmp = jnp.einsum('bgrsd,bgnd->bgrsn', q, kc) * scale
    p_cmp = masked_softmax(s_cmp, cmp_mask)
    o_cmp = jnp.einsum('bgrsn,bgnd->bgrsd', p_cmp.astype(vc.dtype), vc)

    n_sel = s // SEL_BLOCK
    n_pick = min(SEL_TOPK, n_sel)
    imp = jnp.einsum('bgsn,nj->bgsj', p_cmp.sum(axis=2), cmp_to_sel_matrix(n_cmp, n_sel))
    blk = jnp.arange(n_sel)[None, :]
    cur = (t_pos // SEL_BLOCK)[:, None]
    forced = (blk == 0) | (blk == cur) | (blk == cur - 1)
    score = jnp.where(forced, FORCE_SCORE, jnp.where(blk <= cur, imp, NEG_INF))
    top_val, sel_idx = lax.top_k(score, n_pick)
    sel_ok = top_val > 0.5 * NEG_INF
    ks = apply_rope(heads(k_sel).transpose(0, 2, 1, 3), cos, sin).reshape(b, G, n_sel, SEL_BLOCK, DH)
    vs = heads(v_sel).transpose(0, 2, 1, 3).reshape(b, G, n_sel, SEL_BLOCK, DH)
    bi = jnp.arange(b)[:, None, None, None]
    gi = jnp.arange(G)[None, :, None, None]
    n_qc = s // SEL_Q_CHUNK

    def sel_chunk(args):
        c, qc, ic, okc = args
        kg = ks[bi, gi, ic]
        vg = vs[bi, gi, ic]
        sc = jnp.einsum('bgrqd,bgqnkd->bgrqnk', qc, kg) * scale
        tq = c * SEL_Q_CHUNK + jnp.arange(SEL_Q_CHUNK)
        kpos = ic[..., None] * SEL_BLOCK + jnp.arange(SEL_BLOCK)
        m = (kpos <= tq[None, None, :, None, None]) & okc[..., None]
        m = m[:, :, None].reshape(b, G, 1, SEL_Q_CHUNK, n_pick * SEL_BLOCK)
        p = masked_softmax(sc.reshape(b, G, R, SEL_Q_CHUNK, n_pick * SEL_BLOCK), m)
        p = p.reshape(sc.shape).astype(vg.dtype)
        return jnp.einsum('bgrqnk,bgqnkd->bgrqd', p, vg)

    q_chunks = jnp.moveaxis(q_rot.reshape(b, G, R, n_qc, SEL_Q_CHUNK, DH), 3, 0)
    i_chunks = jnp.moveaxis(sel_idx.reshape(b, G, n_qc, SEL_Q_CHUNK, n_pick), 2, 0)
    ok_chunks = jnp.moveaxis(sel_ok.reshape(b, G, n_qc, SEL_Q_CHUNK, n_pick), 2, 0)
    o_sel = lax.map(sel_chunk, (jnp.arange(n_qc), q_chunks, i_chunks, ok_chunks))
    o_sel = jnp.moveaxis(o_sel, 0, 3).reshape(b, G, R, s, DH)

    kw = apply_rope(heads(k_win).transpose(0, 2, 1, 3), cos, sin)
    vw = heads(v_win).transpose(0, 2, 1, 3)
    pad = ((0, 0), (0, 0), (WINDOW, 0), (0, 0))
    kw_p = jnp.pad(kw, pad)
    vw_p = jnp.pad(vw, pad)

    def win_block(qb):
        start = qb * Q_BLOCK
        qblk = lax.dynamic_slice_in_dim(q_rot, start, Q_BLOCK, axis=3)
        kblk = lax.dynamic_slice_in_dim(kw_p, start, Q_BLOCK + WINDOW, axis=2)
        vblk = lax.dynamic_slice_in_dim(vw_p, start, Q_BLOCK + WINDOW, axis=2)
        sc = jnp.einsum('bgrqd,bgkd->bgrqk', qblk, kblk) * scale
        tq = (start + jnp.arange(Q_BLOCK))[:, None]
        kp = (start - WINDOW + jnp.arange(Q_BLOCK + WINDOW))[None, :]
        m = (kp >= 0) & (kp <= tq) & (kp > tq - WINDOW)
        p = masked_softmax(sc, m).astype(vblk.dtype)
        return jnp.einsum('bgrqk,bgkd->bgrqd', p, vblk)

    o_win = lax.map(win_block, jnp.arange(s // Q_BLOCK))
    o_win = jnp.moveaxis(o_win, 0, 3).reshape(b, G, R, s, DH)

    g = jax.nn.sigmoid(g_raw.reshape(b, s, G, R, 3)).transpose(0, 2, 3, 1, 4)
    o = g[..., 0:1] * o_cmp + g[..., 1:2] * o_sel + g[..., 2:3] * o_win
    return o.transpose(0, 3, 1, 2, 4).reshape(b, s, Q_W)


def rglru_mixer(x_in, y_in, conv_w, conv_b, wa, ba, wi, bi, lam):
    b, s, w = x_in.shape
    xp = jnp.pad(x_in, ((0, 0), (CONV_WIDTH - 1, 0), (0, 0)))
    xc = conv_b
    for k in range(CONV_WIDTH):
        xc = xc + xp[:, k:k + s] * conv_w[k]
    xb = xc.reshape(b, s, LRU_BLOCKS, LRU_BLOCK_W)
    r = jax.nn.sigmoid(jnp.einsum('bsnc,ncd->bsnd', xb, wa) + ba).reshape(b, s, w)
    i = jax.nn.sigmoid(jnp.einsum('bsnc,ncd->bsnd', xb, wi) + bi).reshape(b, s, w)
    log_a = -LRU_C * jax.nn.softplus(-lam.astype(jnp.float32)) * r.astype(jnp.float32)
    a = jnp.exp(log_a)
    u = jnp.sqrt(-jnp.expm1(2.0 * log_a)) * (i * xc).astype(jnp.float32)

    def combine(left, right):
        a1, b1 = left
        a2, b2 = right
        return a1 * a2, a2 * b1 + b2

    _, h = lax.associative_scan(combine, (a, u), axis=1)
    return h.astype(x_in.dtype) * jax.nn.gelu(y_in)


def moe_ffn(h, router_w, router_b, w_gate, b_gate, w_up, b_up, w_down, b_down):
    b, s, d = h.shape
    T = b * s
    A = T * TOP_K
    C = MOE_ROW_BLOCK
    ht = h.reshape(T, d)
    logits = (ht @ router_w + router_b).astype(jnp.float32)
    top_logit, top_e = lax.top_k(logits, TOP_K)
    gate = jax.nn.softmax(top_logit, axis=-1)
    e_flat = top_e.reshape(A)
    order = jnp.argsort(e_flat)
    e_sorted = e_flat[order]
    counts = jnp.zeros((N_EXPERTS,), jnp.int32).at[e_flat].add(1)
    starts = jnp.cumsum(counts) - counts
    padded = (counts + C - 1) // C * C
    pend = jnp.cumsum(padded)
    pstart = pend - padded
    dest = pstart[e_sorted] + (jnp.arange(A) - starts[e_sorted])
    n_rows = A + N_EXPERTS * C
    n_chunks = n_rows // C
    row_tok = jnp.full((n_rows,), T, jnp.int32).at[dest].set((order // TOP_K).astype(jnp.int32))
    row_w = jnp.zeros((n_rows,), jnp.float32).at[dest].set(gate.reshape(A)[order])
    x_rows = jnp.concatenate([ht, jnp.zeros((1, d), ht.dtype)], axis=0)[row_tok]
    chunk_e = jnp.minimum(jnp.sum((jnp.arange(n_chunks) * C)[:, None] >= pend[None, :], axis=1), N_EXPERTS - 1)

    def expert_block(args):
        xr, wr, e = args
        g = xr @ w_gate[e] + b_gate[e]
        u = xr @ w_up[e] + b_up[e]
        g = jnp.minimum(g, SWIGLU_LIMIT)
        u = jnp.clip(u, -SWIGLU_LIMIT, SWIGLU_LIMIT)
        act = (u + 1.0) * (g * jax.nn.sigmoid(SWIGLU_ALPHA * g))
        return (act @ w_down[e] + b_down[e]) * wr[:, None].astype(xr.dtype)

    y_rows = lax.map(expert_block, (x_rows.reshape(n_chunks, C, d), row_w.reshape(n_chunks, C), chunk_e))
    out = jax.ops.segment_sum(y_rows.reshape(n_rows, d), row_tok, num_segments=T + 1)[:T]
    return out.reshape(b, s, d)


def hybrid_layer(x, norm1_g, w_in, cmp_pe_k, cmp_pe_v, cmp_k_w1, cmp_k_w2, cmp_v_w1, cmp_v_w2,
                 lru_conv_w, lru_conv_b, lru_wa, lru_ba, lru_wi, lru_bi, lru_lambda,
                 w_br_attn, w_br_lru, w_out, norm2_g, router_w, router_b,
                 exp_w_gate, exp_b_gate, exp_w_up, exp_b_up, exp_w_down, exp_b_down):
    h = rms_norm(x, norm1_g)
    proj = h @ w_in
    cuts = np.cumsum(IN_SPLITS)[:-1].tolist()
    q, kc, vc, ks, vs, kw, vw, g_nsa, lx, ly, ga, gb = jnp.split(proj, cuts, axis=-1)
    o_a = nsa_mixer(q, kc, vc, ks, vs, kw, vw, g_nsa, cmp_pe_k, cmp_pe_v, cmp_k_w1, cmp_k_w2, cmp_v_w1, cmp_v_w2)
    o_b = rglru_mixer(lx, ly, lru_conv_w, lru_conv_b, lru_wa, lru_ba, lru_wi, lru_bi, lru_lambda)
    merged = jax.nn.sigmoid(ga) * (o_a @ w_br_attn) + jax.nn.sigmoid(gb) * (o_b @ w_br_lru)
    x = x + merged @ w_out
    x = x + moe_ffn(rms_norm(x, norm2_g), router_w, router_b, exp_w_gate, exp_b_gate,
                    exp_w_up, exp_b_up, exp_w_down, exp_b_down)
    return x


def setup_inputs(seed: int = 0) -> dict:
    key = jax.random.key(seed)
    ks = jax.random.split(key, 32)
    L = DEPTH

    def nrm(k, shape, scale):
        return jax.random.normal(k, shape, jnp.float32) * scale

    u = jax.random.uniform(ks[15], (L, LRU_WIDTH), jnp.float32, minval=0.9, maxval=0.999)
    a0 = u ** (1.0 / LRU_C)
    return {
        "x": nrm(ks[0], (BATCH, SEQ, D_MODEL), 1.0),
        "norm1_g": 1.0 + nrm(ks[1], (L, D_MODEL), 0.02),
        "w_in": nrm(ks[2], (L, D_MODEL, IN_WIDTH), D_MODEL ** -0.5),
        "cmp_pe_k": nrm(ks[3], (L, CMP_BLOCK, HEAD_DIM), 0.02),
        "cmp_pe_v": nrm(ks[4], (L, CMP_BLOCK, HEAD_DIM), 0.02),
        "cmp_k_w1": nrm(ks[5], (L, CMP_BLOCK * HEAD_DIM, CMP_HIDDEN), (CMP_BLOCK * HEAD_DIM) ** -0.5),
        "cmp_k_w2": nrm(ks[6], (L, CMP_HIDDEN, HEAD_DIM), CMP_HIDDEN ** -0.5),
        "cmp_v_w1": nrm(ks[7], (L, CMP_BLOCK * HEAD_DIM, CMP_HIDDEN), (CMP_BLOCK * HEAD_DIM) ** -0.5),
        "cmp_v_w2": nrm(ks[8], (L, CMP_HIDDEN, HEAD_DIM), CMP_HIDDEN ** -0.5),
        "lru_conv_w": nrm(ks[9], (L, CONV_WIDTH, LRU_WIDTH), CONV_WIDTH ** -0.5),
        "lru_conv_b": nrm(ks[10], (L, LRU_WIDTH), 0.01),
        "lru_wa": nrm(ks[11], (L, LRU_BLOCKS, LRU_BLOCK_W, LRU_BLOCK_W), LRU_BLOCK_W ** -0.5),
        "lru_ba": nrm(ks[12], (L, LRU_BLOCKS, LRU_BLOCK_W), 0.01),
        "lru_wi": nrm(ks[13], (L, LRU_BLOCKS, LRU_BLOCK_W, LRU_BLOCK_W), LRU_BLOCK_W ** -0.5),
        "lru_bi": nrm(ks[14], (L, LRU_BLOCKS, LRU_BLOCK_W), 0.01),
        "lru_lambda": jnp.log(a0) - jnp.log1p(-a0),
        "w_br_attn": nrm(ks[16], (L, Q_W, D_MODEL), Q_W ** -0.5),
        "w_br_lru": nrm(ks[17], (L, LRU_WIDTH, D_MODEL), LRU_WIDTH ** -0.5),
        "w_out": nrm(ks[18], (L, D_MODEL, D_MODEL), D_MODEL ** -0.5),
        "norm2_g": 1.0 + nrm(ks[19], (L, D_MODEL), 0.02),
        "router_w": nrm(ks[20], (L, D_MODEL, N_EXPERTS), D_MODEL ** -0.5),
        "router_b": nrm(ks[21], (L, N_EXPERTS), 0.01),
        "exp_w_gate": nrm(ks[22], (L, N_EXPERTS, D_MODEL, D_FF), D_MODEL ** -0.5),
        "exp_b_gate": nrm(ks[23], (L, N_EXPERTS, D_FF), 0.01),
        "exp_w_up": nrm(ks[24], (L, N_EXPERTS, D_MODEL, D_FF), D_MODEL ** -0.5),
        "exp_b_up": nrm(ks[25], (L, N_EXPERTS, D_FF), 0.01),
        "exp_w_down": nrm(ks[26], (L, N_EXPERTS, D_FF, D_MODEL), D_FF ** -0.5),
        "exp_b_down": nrm(ks[27], (L, N_EXPERTS, D_MODEL), 0.01),
        "normf_g": 1.0 + nrm(ks[28], (D_MODEL,), 0.02),
    }


def reference(x, norm1_g, w_in, cmp_pe_k, cmp_pe_v, cmp_k_w1, cmp_k_w2, cmp_v_w1, cmp_v_w2,
              lru_conv_w, lru_conv_b, lru_wa, lru_ba, lru_wi, lru_bi, lru_lambda,
              w_br_attn, w_br_lru, w_out, norm2_g, router_w, router_b,
              exp_w_gate, exp_b_gate, exp_w_up, exp_b_up, exp_w_down, exp_b_down, normf_g):
    for layer in range(DEPTH):
        x = hybrid_layer(x, norm1_g[layer], w_in[layer], cmp_pe_k[layer], cmp_pe_v[layer],
                         cmp_k_w1[layer], cmp_k_w2[layer], cmp_v_w1[layer], cmp_v_w2[layer],
                         lru_conv_w[layer], lru_conv_b[layer], lru_wa[layer], lru_ba[layer],
                         lru_wi[layer], lru_bi[layer], lru_lambda[layer],
                         w_br_attn[layer], w_br_lru[layer], w_out[layer], norm2_g[layer],
                         router_w[layer], router_b[layer], exp_w_gate[layer], exp_b_gate[layer],
                         exp_w_up[layer], exp_b_up[layer], exp_w_down[layer], exp_b_down[layer])
    return rms_norm(x, normf_g)
```

```python
import functools

import numpy as np
import jax
import jax.numpy as jnp
from jax import lax
from jax.experimental import pallas as pl
from jax.experimental.pallas import tpu as pltpu

F32 = jnp.float32
BF16 = jnp.bfloat16
I32 = jnp.int32

D_MODEL = 2048
HEAD_DIM = 128
NSA_Q_HEADS = 16
NSA_KV_GROUPS = 4
NSA_GROUP = NSA_Q_HEADS // NSA_KV_GROUPS
CMP_BLOCK = 32
CMP_STRIDE = 16
CMP_HIDDEN = 256
SEL_BLOCK = 64
SEL_BLOCK_LOG2 = 6
SEL_TOPK = 16
WINDOW = 512
ROPE_THETA = 10000.0
LRU_WIDTH = 2048
LRU_BLOCKS = 16
LRU_BLOCK_W = LRU_WIDTH // LRU_BLOCKS
CONV_WIDTH = 4
LRU_C = 8.0
N_EXPERTS = 32
TOP_K = 4
TOP_K_LOG2 = 2
D_FF = 2048
SWIGLU_ALPHA = 1.702
SWIGLU_LIMIT = 7.0
NORM_EPS = 1e-6
NEG_INF = -1e30
FORCE_SCORE = 1e6

Q_W = NSA_Q_HEADS * HEAD_DIM
KV_W = NSA_KV_GROUPS * HEAD_DIM
GROUP_W = NSA_GROUP * HEAD_DIM
SCALE = HEAD_DIM ** -0.5
LANES = 128
SUBLANES = 8

TM = 512
TN = 512
TQ_CMP = 256
TQ = 128
TK_SEL = 256
WIN_KEYS = WINDOW + TQ
TS_LRU = 256
CW_LRU = 512
TM_OUT = 256
MOE_ROWS = 256
FF_CHUNK = 512
TB_COMBINE = 128
VMEM_LIMIT = 56 * 1024 * 1024


def _params(sem, vmem=None):
    return pltpu.CompilerParams(dimension_semantics=sem, vmem_limit_bytes=vmem)


def _dot(a, b):
    return jnp.dot(a, b, preferred_element_type=F32)


def _dot_nt(a, b, precision=None):
    return lax.dot_general(a, b, (((1,), (1,)), ((), ())), precision=precision, preferred_element_type=F32)


def _norm1_kernel(x_ref, g_ref, wg_ref, h_ref, gate_ref):
    x = x_ref[...]
    y = x * lax.rsqrt(jnp.mean(x * x, axis=-1, keepdims=True) + NORM_EPS) * g_ref[...]
    hb = y.astype(BF16)
    h_ref[...] = hb
    gate_ref[...] = jax.nn.sigmoid(_dot(hb, wg_ref[...]))


def _norm1(x2, g, wg):
    t = x2.shape[0]
    return pl.pallas_call(
        _norm1_kernel,
        grid=(t // TM,),
        in_specs=[
            pl.BlockSpec((TM, D_MODEL), lambda i: (i, 0)),
            pl.BlockSpec((1, D_MODEL), lambda i: (0, 0)),
            pl.BlockSpec((D_MODEL, NSA_KV_GROUPS * LANES), lambda i: (0, 0)),
        ],
        out_specs=[
            pl.BlockSpec((TM, D_MODEL), lambda i: (i, 0)),
            pl.BlockSpec((TM, NSA_KV_GROUPS * LANES), lambda i: (i, 0)),
        ],
        out_shape=[
            jax.ShapeDtypeStruct((t, D_MODEL), BF16),
            jax.ShapeDtypeStruct((t, NSA_KV_GROUPS * LANES), F32),
        ],
        compiler_params=_params(("parallel",)),
        name="norm1",
    )(x2, g, wg)


def _rope(x, cos, sin_signed):
    return x * cos + pltpu.roll(x, HEAD_DIM // 2, axis=1) * sin_signed


def _qproj_kernel(h_ref, w_ref, cos_ref, sin_ref, qraw_ref, qrot_ref):
    acc = _dot(h_ref[...], w_ref[...])
    qraw_ref[...] = acc.astype(BF16)
    cos = cos_ref[...]
    sin = sin_ref[...]
    for hh in range(TN // HEAD_DIM):
        sl = slice(hh * HEAD_DIM, (hh + 1) * HEAD_DIM)
        qrot_ref[:, sl] = _rope(acc[:, sl], cos, sin).astype(BF16)


def _qproj(h, wq, cos, sin, s):
    t = h.shape[0]
    ns = s // TM
    return pl.pallas_call(
        _qproj_kernel,
        grid=(Q_W // TN, t // TM),
        in_specs=[
            pl.BlockSpec((TM, D_MODEL), lambda j, i: (i, 0)),
            pl.BlockSpec((D_MODEL, TN), lambda j, i: (0, j)),
            pl.BlockSpec((TM, HEAD_DIM), lambda j, i: (i % ns, 0)),
            pl.BlockSpec((TM, HEAD_DIM), lambda j, i: (i % ns, 0)),
        ],
        out_specs=[
            pl.BlockSpec((TM, TN), lambda j, i: (i, j)),
            pl.BlockSpec((TM, TN), lambda j, i: (i, j)),
        ],
        out_shape=[jax.ShapeDtypeStruct((t, Q_W), BF16), jax.ShapeDtypeStruct((t, Q_W), BF16)],
        compiler_params=_params(("parallel", "parallel")),
        name="qproj",
    )(h, wq, cos, sin)


KV_KINDS = 6
KIND_K_SEL = 2
KIND_K_WIN = 4


def _kvproj_kernel(h_ref, w_ref, cos_ref, sin_ref, out_ref):
    kind = pl.program_id(0)
    acc = _dot(h_ref[...], w_ref[...])
    cos = cos_ref[...]
    sin = sin_ref[...]
    is_rope = jnp.logical_or(kind == KIND_K_SEL, kind == KIND_K_WIN)
    for g in range(NSA_KV_GROUPS):
        xh = acc[:, g * HEAD_DIM:(g + 1) * HEAD_DIM]
        out_ref[0, 0, g] = jnp.where(is_rope, _rope(xh, cos, sin), xh).astype(BF16)


def _kvproj(h, wkv, cos, sin, b, s):
    t = h.shape[0]
    ns = s // TM
    return pl.pallas_call(
        _kvproj_kernel,
        grid=(KV_KINDS, t // TM),
        in_specs=[
            pl.BlockSpec((TM, D_MODEL), lambda k, i: (i, 0)),
            pl.BlockSpec((D_MODEL, KV_W), lambda k, i: (0, k)),
            pl.BlockSpec((TM, HEAD_DIM), lambda k, i: (i % ns, 0)),
            pl.BlockSpec((TM, HEAD_DIM), lambda k, i: (i % ns, 0)),
        ],
        out_specs=pl.BlockSpec((1, 1, NSA_KV_GROUPS, TM, HEAD_DIM), lambda k, i: (k, i // ns, 0, i % ns, 0)),
        out_shape=jax.ShapeDtypeStruct((KV_KINDS, b, NSA_KV_GROUPS, s, HEAD_DIM), BF16),
        compiler_params=_params(("parallel", "parallel")),
        name="kvproj",
    )(h, wkv, cos, sin)


def _proj_kernel(h_ref, w_ref, o_ref):
    o_ref[...] = _dot(h_ref[...], w_ref[...]).astype(o_ref.dtype)


def _proj(h, w, name):
    t = h.shape[0]
    n = w.shape[1]
    return pl.pallas_call(
        _proj_kernel,
        grid=(n // TN, t // TM),
        in_specs=[
            pl.BlockSpec((TM, D_MODEL), lambda j, i: (i, 0)),
            pl.BlockSpec((D_MODEL, TN), lambda j, i: (0, j)),
        ],
        out_specs=pl.BlockSpec((TM, TN), lambda j, i: (i, j)),
        out_shape=jax.ShapeDtypeStruct((t, n), BF16),
        compiler_params=_params(("parallel", "parallel")),
        name=name,
    )(h, w)


def _cmpkv_kernel(c_ref, w1_ref, w2_ref, pe_ref, o_ref):
    c = c_ref[0, 0, 0]
    half = CMP_STRIDE * HEAD_DIM
    top = _dot(c, w1_ref[0, :half, :])
    bot = _dot(c, w1_ref[0, half:, :])
    n_rows = top.shape[0]
    bot_next = pltpu.roll(bot, n_rows - 1, axis=0)
    bias = _dot(pe_ref[0], w1_ref[0])[0:1, :]
    hid = jax.nn.gelu(top + bot_next + bias)
    o_ref[0, 0, 0] = _dot(hid.astype(BF16), w2_ref[0]).astype(BF16)


def _cmpkv(kv6, w1, w2, pe, b, s):
    nc = s // CMP_STRIDE
    kvc = kv6.reshape(KV_KINDS, b, NSA_KV_GROUPS, nc, CMP_STRIDE * HEAD_DIM)
    return pl.pallas_call(
        _cmpkv_kernel,
        grid=(2, b, NSA_KV_GROUPS),
        in_specs=[
            pl.BlockSpec((1, 1, 1, nc, CMP_STRIDE * HEAD_DIM), lambda k, bi, g: (k, bi, g, 0, 0)),
            pl.BlockSpec((1, CMP_BLOCK * HEAD_DIM, CMP_HIDDEN), lambda k, bi, g: (k, 0, 0)),
            pl.BlockSpec((1, CMP_HIDDEN, HEAD_DIM), lambda k, bi, g: (k, 0, 0)),
            pl.BlockSpec((1, SUBLANES, CMP_BLOCK * HEAD_DIM), lambda k, bi, g: (k, 0, 0)),
        ],
        out_specs=pl.BlockSpec((1, 1, 1, nc, HEAD_DIM), lambda k, bi, g: (k, bi, g, 0, 0)),
        out_shape=jax.ShapeDtypeStruct((2, b, NSA_KV_GROUPS, nc, HEAD_DIM), BF16),
        compiler_params=_params(("parallel", "parallel", "parallel")),
        name="cmpkv",
    )(kvc, w1, w2, pe)


def _cmpattn_kernel(q_ref, kc_ref, vc_ref, mt_ref, ocmp_ref, msk_ref, *, n_sel):
    i = pl.program_id(2)
    tq = TQ_CMP
    q = q_ref[0]
    kc = kc_ref[0, 0, 0]
    vc = vc_ref[0, 0, 0]
    nc = kc.shape[0]
    t_row = i * tq + lax.broadcasted_iota(I32, (tq, 1), 0)
    n_lane = lax.broadcasted_iota(I32, (tq, nc), 1)
    cmask = (n_lane * CMP_STRIDE + (CMP_BLOCK - 1)) <= t_row
    psum = jnp.zeros((tq, nc), F32)
    for hh in range(NSA_GROUP):
        sl = slice(hh * HEAD_DIM, (hh + 1) * HEAD_DIM)
        s = _dot_nt(q[:, sl], kc) * SCALE
        s = jnp.where(cmask, s, NEG_INF)
        e = jnp.exp(s - jnp.max(s, axis=-1, keepdims=True))
        p = e / jnp.sum(e, axis=-1, keepdims=True)
        p = jnp.where(cmask, p, 0.0)
        ocmp_ref[0, :, sl] = _dot(p.astype(BF16), vc).astype(BF16)
        psum = psum + p
    imp_t = _dot_nt(mt_ref[...], psum, precision=lax.Precision.HIGHEST)
    imp_t = imp_t[:n_sel]
    jb = lax.broadcasted_iota(I32, (n_sel, tq), 0)
    cur = (i * tq + lax.broadcasted_iota(I32, (n_sel, tq), 1)) >> SEL_BLOCK_LOG2
    forced = (jb == 0) | (jb == cur) | (jb == cur - 1)
    valid = jb <= cur
    score = jnp.where(forced, FORCE_SCORE, jnp.where(valid, imp_t, NEG_INF))
    rank = jnp.zeros((n_sel, tq), I32)
    for jp in range(n_sel):
        row = score[jp:jp + 1, :]
        beats = (row > score) | ((row == score) & (jb > jp))
        rank = rank + beats.astype(I32)
    member = (valid & (rank < SEL_TOPK)).astype(F32)
    member = jnp.concatenate([member, jnp.zeros((LANES - n_sel, tq), F32)], axis=0)
    msk_ref[0, 0] = member.T.astype(BF16)


def _cmpattn(q_raw, kvc, mt, b, s):
    nc = s // CMP_STRIDE
    n_sel = s // SEL_BLOCK
    q3 = q_raw.reshape(b, s, Q_W)
    return pl.pallas_call(
        functools.partial(_cmpattn_kernel, n_sel=n_sel),
        grid=(b, NSA_KV_GROUPS, s // TQ_CMP),
        in_specs=[
            pl.BlockSpec((1, TQ_CMP, GROUP_W), lambda bi, g, i: (bi, i, g)),
            pl.BlockSpec((1, 1, 1, nc, HEAD_DIM), lambda bi, g, i: (0, bi, g, 0, 0)),
            pl.BlockSpec((1, 1, 1, nc, HEAD_DIM), lambda bi, g, i: (1, bi, g, 0, 0)),
            pl.BlockSpec((LANES, nc), lambda bi, g, i: (0, 0)),
        ],
        out_specs=[
            pl.BlockSpec((1, TQ_CMP, GROUP_W), lambda bi, g, i: (bi, i, g)),
            pl.BlockSpec((1, 1, TQ_CMP, LANES), lambda bi, g, i: (bi, g, i, 0)),
        ],
        out_shape=[
            jax.ShapeDtypeStruct((b, s, Q_W), BF16),
            jax.ShapeDtypeStruct((b, NSA_KV_GROUPS, s, LANES), BF16),
        ],
        compiler_params=_params(("parallel", "parallel", "parallel")),
        name="cmpattn",
    )(q3, kvc, kvc, mt)


def _selwin_kernel(q_ref, ks_ref, vs_ref, kw_ref, vw_ref, msk_ref, emat_ref, ocmp_ref, gate_ref, o_ref,
                   m_sc, l_sc, acc_sc, p_sc, *, seq):
    i = pl.program_id(2)
    tq = TQ
    q = q_ref[0]
    q4 = jnp.concatenate([q[:, hh * HEAD_DIM:(hh + 1) * HEAD_DIM] for hh in range(NSA_GROUP)], axis=0)
    t_row = i * tq + lax.broadcasted_iota(I32, (tq, 1), 0)

    m_sc[...] = jnp.full(m_sc.shape, NEG_INF, F32)
    l_sc[...] = jnp.zeros(l_sc.shape, F32)
    acc_sc[...] = jnp.zeros(acc_sc.shape, F32)
    selm = msk_ref[0, 0]
    n_kv = (i * tq + tq - 1) // TK_SEL + 1

    def sel_step(j, carry):
        off = pl.multiple_of(j * TK_SEL, TK_SEL)
        k = ks_ref[0, 0, 0, pl.ds(off, TK_SEL), :]
        v = vs_ref[0, 0, 0, pl.ds(off, TK_SEL), :]
        s = _dot_nt(q4, k) * SCALE
        picked = _dot(selm, emat_ref[j])
        kpos = off + lax.broadcasted_iota(I32, (tq, TK_SEL), 1)
        ok = (picked > 0.5) & (kpos <= t_row)
        for hh in range(NSA_GROUP):
            rows = slice(hh * tq, (hh + 1) * tq)
            sh = jnp.where(ok, s[rows], NEG_INF)
            m_old = m_sc[rows]
            m_new = jnp.maximum(m_old, jnp.max(sh, axis=-1, keepdims=True))
            alpha = jnp.exp(m_old - m_new)
            p = jnp.where(ok, jnp.exp(sh - m_new), 0.0)
            l_sc[rows] = alpha * l_sc[rows] + jnp.sum(p, axis=-1, keepdims=True)
            acc_sc[rows] = alpha * acc_sc[rows]
            m_sc[rows] = m_new
            p_sc[rows] = p.astype(BF16)
        acc_sc[...] += _dot(p_sc[...], v)
        return carry

    lax.fori_loop(0, n_kv, sel_step, 0)

    start = jnp.clip((i - WINDOW // tq) * tq, 0, seq - WIN_KEYS)
    start = pl.multiple_of(start, tq)
    kw = kw_ref[0, 0, 0, pl.ds(start, WIN_KEYS), :]
    vw = vw_ref[0, 0, 0, pl.ds(start, WIN_KEYS), :]
    sw = _dot_nt(q4, kw) * SCALE
    kp = start + lax.broadcasted_iota(I32, (tq, WIN_KEYS), 1)
    okw = (kp <= t_row) & (kp > t_row - WINDOW)

    gate = gate_ref[0]
    ocmp = ocmp_ref[0]
    for hh in range(NSA_GROUP):
        rows = slice(hh * tq, (hh + 1) * tq)
        cols = slice(hh * HEAD_DIM, (hh + 1) * HEAD_DIM)
        swh = jnp.where(okw, sw[rows], NEG_INF)
        ew = jnp.where(okw, jnp.exp(swh - jnp.max(swh, axis=-1, keepdims=True)), 0.0)
        lw = jnp.sum(ew, axis=-1, keepdims=True)
        o_win = _dot(ew.astype(BF16), vw) * jnp.where(lw > 0.0, 1.0 / lw, 0.0)
        ls = l_sc[rows]
        o_sel = acc_sc[rows] * jnp.where(ls > 0.0, 1.0 / ls, 0.0)
        g_cmp = gate[:, 3 * hh:3 * hh + 1]
        g_sel = gate[:, 3 * hh + 1:3 * hh + 2]
        g_win = gate[:, 3 * hh + 2:3 * hh + 3]
        o = g_cmp * ocmp[:, cols].astype(F32) + g_sel * o_sel + g_win * o_win
        o_ref[0, :, cols] = o.astype(BF16)


def _selwin(q_rot, kv6, msk, emat, ocmp, gates, b, s):
    q3 = q_rot.reshape(b, s, Q_W)
    g3 = gates.reshape(b, s, NSA_KV_GROUPS * LANES)

    def kv_spec(kind):
        return pl.BlockSpec((1, 1, 1, s, HEAD_DIM), lambda bi, g, i: (kind, bi, g, 0, 0))

    return pl.pallas_call(
        functools.partial(_selwin_kernel, seq=s),
        grid=(b, NSA_KV_GROUPS, s // TQ),
        in_specs=[
            pl.BlockSpec((1, TQ, GROUP_W), lambda bi, g, i: (bi, i, g)),
            kv_spec(2), kv_spec(3), kv_spec(4), kv_spec(5),
            pl.BlockSpec((1, 1, TQ, LANES), lambda bi, g, i: (bi, g, i, 0)),
            pl.BlockSpec((s // TK_SEL, LANES, TK_SEL), lambda bi, g, i: (0, 0, 0)),
            pl.BlockSpec((1, TQ, GROUP_W), lambda bi, g, i: (bi, i, g)),
            pl.BlockSpec((1, TQ, LANES), lambda bi, g, i: (bi, i, g)),
        ],
        out_specs=pl.BlockSpec((1, TQ, GROUP_W), lambda bi, g, i: (bi, i, g)),
        out_shape=jax.ShapeDtypeStruct((b, s, Q_W), BF16),
        scratch_shapes=[
            pltpu.VMEM((NSA_GROUP * TQ, 1), F32),
            pltpu.VMEM((NSA_GROUP * TQ, 1), F32),
            pltpu.VMEM((NSA_GROUP * TQ, HEAD_DIM), F32),
            pltpu.VMEM((NSA_GROUP * TQ, TK_SEL), BF16),
        ],
        compiler_params=_params(("parallel", "parallel", "parallel")),
        name="selwin",
    )(q3, kv6, kv6, kv6, kv6, msk, emat, ocmp, g3)


def _softplus(x):
    return jnp.maximum(x, 0.0) + jnp.log1p(jnp.exp(-jnp.abs(x)))


def _lru_kernel(lx_ref, ly_ref, cw_ref, cb_ref, wa_ref, ba_ref, wi_ref, bi_ref, lam_ref, o_ref, xe_sc, h_sc):
    ts = TS_LRU
    halo = SUBLANES

    @pl.when(pl.program_id(2) == 0)
    def _():
        xe_sc[0:halo, :] = jnp.zeros((halo, CW_LRU), F32)
        h_sc[...] = jnp.zeros(h_sc.shape, F32)

    x = lx_ref[0].astype(F32)
    xe_sc[halo:halo + ts, :] = x
    xc = cb_ref[...]
    for k in range(CONV_WIDTH):
        d = CONV_WIDTH - 1 - k
        xc = xc + xe_sc[pl.ds(halo - d, ts), :] * cw_ref[k:k + 1, :]
    xe_sc[0:halo, :] = x[ts - halo:ts]

    row_in_group = lax.broadcasted_iota(I32, (ts, LRU_BLOCK_W), 0) & (SUBLANES - 1)
    for n in range(CW_LRU // LRU_BLOCK_W):
        cols = slice(n * LRU_BLOCK_W, (n + 1) * LRU_BLOCK_W)
        xcn = xc[:, cols]
        xb = xcn.astype(BF16)
        r = jax.nn.sigmoid(_dot(xb, wa_ref[n]) + ba_ref[:, cols])
        ig = jax.nn.sigmoid(_dot(xb, wi_ref[n]) + bi_ref[:, cols])
        log_a = (-LRU_C * _softplus(-lam_ref[:, cols])) * r
        a = jnp.exp(log_a)
        u = jnp.sqrt(-jnp.tanh(log_a) * (a * a + 1.0)) * (ig * xcn)
        ca, cb = a, u
        for d in (1, 2, 4):
            ok = row_in_group >= d
            ca_sh = pltpu.roll(ca, d, axis=0)
            cb_sh = pltpu.roll(cb, d, axis=0)
            cb = jnp.where(ok, ca * cb_sh + cb, cb)
            ca = jnp.where(ok, ca * ca_sh, ca)
        hprev = h_sc[:, cols]
        hs = []
        for gi in range(ts // SUBLANES):
            rows = slice(gi * SUBLANES, (gi + 1) * SUBLANES)
            hg = ca[rows] * hprev + cb[rows]
            hs.append(hg)
            hprev = hg[SUBLANES - 1:SUBLANES]
        h_sc[:, cols] = hprev
        h = jnp.concatenate(hs, axis=0)
        o_ref[0, :, cols] = (h * jax.nn.gelu(ly_ref[0, :, cols].astype(F32))).astype(BF16)


def _lru(lxy, conv_w, conv_b, wa, ba, wi, bi, lam, b, s):
    lxy3 = lxy.reshape(b, s, 2 * LRU_WIDTH)
    ncw = LRU_WIDTH // CW_LRU
    nb = CW_LRU // LRU_BLOCK_W
    vec = pl.BlockSpec((1, CW_LRU), lambda bi_, c, t: (0, c))
    return pl.pallas_call(
        _lru_kernel,
        grid=(b, ncw, s // TS_LRU),
        in_specs=[
            pl.BlockSpec((1, TS_LRU, CW_LRU), lambda bi_, c, t: (bi_, t, c)),
            pl.BlockSpec((1, TS_LRU, CW_LRU), lambda bi_, c, t: (bi_, t, ncw + c)),
            pl.BlockSpec((CONV_WIDTH, CW_LRU), lambda bi_, c, t: (0, c)),
            vec,
            pl.BlockSpec((nb, LRU_BLOCK_W, LRU_BLOCK_W), lambda bi_, c, t: (c, 0, 0)),
            vec,
            pl.BlockSpec((nb, LRU_BLOCK_W, LRU_BLOCK_W), lambda bi_, c, t: (c, 0, 0)),
            vec,
            vec,
        ],
        out_specs=pl.BlockSpec((1, TS_LRU, CW_LRU), lambda bi_, c, t: (bi_, t, c)),
        out_shape=jax.ShapeDtypeStruct((b, s, LRU_WIDTH), BF16),
        scratch_shapes=[
            pltpu.VMEM((TS_LRU + SUBLANES, CW_LRU), F32),
            pltpu.VMEM((1, CW_LRU), F32),
        ],
        compiler_params=_params(("parallel", "parallel", "arbitrary")),
        name="lru",
    )(lxy3, lxy3, conv_w, conv_b, wa, ba, wi, bi, lam)


def _merge_kernel(h_ref, oa_ref, ob_ref, wga_ref, wgb_ref, wpa_ref, wpb_ref, o_ref):
    h = h_ref[...]
    ga = jax.nn.sigmoid(_dot(h, wga_ref[...]))
    gb = jax.nn.sigmoid(_dot(h, wgb_ref[...]))
    o_ref[...] = (ga * _dot(oa_ref[...], wpa_ref[...]) + gb * _dot(ob_ref[...], wpb_ref[...])).astype(BF16)


def _merge(h, oa, ob, wga, wgb, wpa, wpb):
    t = h.shape[0]
    row = pl.BlockSpec((TM, D_MODEL), lambda j, i: (i, 0))
    col = pl.BlockSpec((D_MODEL, TN), lambda j, i: (0, j))
    return pl.pallas_call(
        _merge_kernel,
        grid=(D_MODEL // TN, t // TM),
        in_specs=[row, row, row, col, col, col, col],
        out_specs=pl.BlockSpec((TM, TN), lambda j, i: (i, j)),
        out_shape=jax.ShapeDtypeStruct((t, D_MODEL), BF16),
        compiler_params=_params(("parallel", "parallel"), VMEM_LIMIT),
        name="merge",
    )(h, oa, ob, wga, wgb, wpa, wpb)


def _outrouter_kernel(mg_ref, x_ref, wo_ref, g2_ref, rw_ref, rb_ref, tri_ref,
                      x1_ref, h2_ref, tope_ref, gate_ref, rank_ref, cnt_ref, carry_sc):
    tm = TM_OUT

    @pl.when(pl.program_id(0) == 0)
    def _():
        carry_sc[...] = jnp.zeros(carry_sc.shape, F32)

    x1 = x_ref[...] + _dot(mg_ref[...], wo_ref[...])
    x1_ref[...] = x1
    y = x1 * lax.rsqrt(jnp.mean(x1 * x1, axis=-1, keepdims=True) + NORM_EPS) * g2_ref[...]
    h2_ref[...] = y
    logits = jnp.dot(y, rw_ref[...], precision=lax.Precision.HIGHEST, preferred_element_type=F32) + rb_ref[...]
    lane = lax.broadcasted_iota(I32, (tm, LANES), 1)
    lane_f = lane.astype(F32)
    work = jnp.where(lane < N_EXPERTS, logits, -jnp.inf)
    vals, idxs = [], []
    for _k in range(TOP_K):
        m = jnp.max(work, axis=-1, keepdims=True)
        idx = jnp.min(jnp.where(work == m, lane_f, float(LANES)), axis=-1, keepdims=True).astype(I32)
        vals.append(m)
        idxs.append(idx)
        work = jnp.where(lane == idx, -jnp.inf, work)
    es = [jnp.exp(v - vals[0]) for v in vals]
    denom = es[0] + es[1] + es[2] + es[3]
    cnt = jnp.zeros((tm, LANES), F32)
    for idx in idxs:
        cnt = cnt + (lane == idx).astype(F32)
    before = _dot(tri_ref[...], cnt.astype(BF16)) + carry_sc[...]
    tope = jnp.zeros((tm, LANES), I32)
    gate = jnp.zeros((tm, LANES), F32)
    rank = jnp.zeros((tm, LANES), F32)
    for k in range(TOP_K):
        tope = jnp.where(lane == k, idxs[k], tope)
        gate = jnp.where(lane == k, es[k] / denom, gate)
        rk = jnp.sum(jnp.where(lane == idxs[k], before, 0.0), axis=-1, keepdims=True)
        rank = jnp.where(lane == k, rk, rank)
    tope_ref[...] = tope
    gate_ref[...] = gate
    rank_ref[...] = rank.astype(I32)
    carry = carry_sc[...] + jnp.sum(cnt, axis=0, keepdims=True)
    carry_sc[...] = carry
    cnt_ref[...] = jnp.broadcast_to(carry, cnt_ref.shape).astype(I32)


def _outrouter(merged, x2, wo, g2, rw, rb, tri):
    t = x2.shape[0]
    row_bf = pl.BlockSpec((TM_OUT, D_MODEL), lambda i: (i, 0))
    lane_blk = pl.BlockSpec((TM_OUT, LANES), lambda i: (i, 0))
    return pl.pallas_call(
        _outrouter_kernel,
        grid=(t // TM_OUT,),
        in_specs=[
            row_bf,
            row_bf,
            pl.BlockSpec((D_MODEL, D_MODEL), lambda i: (0, 0)),
            pl.BlockSpec((1, D_MODEL), lambda i: (0, 0)),
            pl.BlockSpec((D_MODEL, LANES), lambda i: (0, 0)),
            pl.BlockSpec((1, LANES), lambda i: (0, 0)),
            pl.BlockSpec((TM_OUT, TM_OUT), lambda i: (0, 0)),
        ],
        out_specs=[row_bf, row_bf, lane_blk, lane_blk, lane_blk,
                   pl.BlockSpec((SUBLANES, LANES), lambda i: (0, 0))],
        out_shape=[
            jax.ShapeDtypeStruct((t, D_MODEL), F32),
            jax.ShapeDtypeStruct((t, D_MODEL), F32),
            jax.ShapeDtypeStruct((t, LANES), I32),
            jax.ShapeDtypeStruct((t, LANES), F32),
            jax.ShapeDtypeStruct((t, LANES), I32),
            jax.ShapeDtypeStruct((SUBLANES, LANES), I32),
        ],
        scratch_shapes=[pltpu.VMEM((1, LANES), F32)],
        compiler_params=_params(("arbitrary",), VMEM_LIMIT),
        name="outrouter",
    )(merged, x2, wo, g2, rw, rb, tri)


def _row_copy(src_hbm, dst, src_row, dst_row, sem):
    return pltpu.make_async_copy(src_hbm.at[pl.ds(src_row, 1)], dst.at[pl.ds(dst_row, 1)], sem)


def _moegather_kernel(na_ref, tok_ref, h_hbm, o_ref, sem):
    c = pl.program_id(0)

    @pl.when(c < na_ref[0])
    def _():
        def issue(r, carry):
            _row_copy(h_hbm, o_ref, tok_ref[0, 0, r], r, sem.at[0]).start()
            return carry

        lax.fori_loop(0, MOE_ROWS, issue, 0)

        def drain(r, carry):
            _row_copy(h_hbm, o_ref, 0, r, sem.at[0]).wait()
            return carry

        lax.fori_loop(0, MOE_ROWS, drain, 0)

    @pl.when(c >= na_ref[0])
    def _():
        o_ref[...] = jnp.zeros(o_ref.shape, o_ref.dtype)


def _moegather(n_active, row_tok, h2, n_chunks):
    grid_spec = pltpu.PrefetchScalarGridSpec(
        num_scalar_prefetch=1,
        grid=(n_chunks,),
        in_specs=[
            pl.BlockSpec((1, 1, MOE_ROWS), lambda c, na: (c, 0, 0), memory_space=pltpu.SMEM),
            pl.BlockSpec(memory_space=pl.ANY),
        ],
        out_specs=pl.BlockSpec((MOE_ROWS, D_MODEL), lambda c, na: (c, 0)),
        scratch_shapes=[pltpu.SemaphoreType.DMA((1,))],
    )
    return pl.pallas_call(
        _moegather_kernel,
        grid_spec=grid_spec,
        out_shape=jax.ShapeDtypeStruct((n_chunks * MOE_ROWS, D_MODEL), F32),
        compiler_params=_params(("arbitrary",)),
        name="moegather",
    )(n_active, row_tok.reshape(n_chunks, 1, MOE_ROWS), h2)


def _experts_kernel(ce_ref, na_ref, x_ref, wg_ref, bg_ref, wu_ref, bu_ref, wd_ref, bd_ref, y_ref):
    c = pl.program_id(0)

    @pl.when(c < na_ref[0])
    def _():
        x = x_ref[...].astype(BF16)
        acc = jnp.zeros((MOE_ROWS, D_MODEL), F32)
        for f in range(D_FF // FF_CHUNK):
            cols = slice(f * FF_CHUNK, (f + 1) * FF_CHUNK)
            g = _dot(x, wg_ref[0, :, cols]) + bg_ref[0, :, cols]
            u = _dot(x, wu_ref[0, :, cols]) + bu_ref[0, :, cols]
            g = jnp.minimum(g, SWIGLU_LIMIT)
            u = jnp.clip(u, -SWIGLU_LIMIT, SWIGLU_LIMIT)
            act = (u + 1.0) * (g * jax.nn.sigmoid(SWIGLU_ALPHA * g))
            acc = acc + _dot(act.astype(BF16), wd_ref[0, cols, :])
        y_ref[...] = acc + bd_ref[0]

    @pl.when(c >= na_ref[0])
    def _():
        y_ref[...] = jnp.zeros(y_ref.shape, y_ref.dtype)


def _experts(chunk_e, n_active, x_rows, wg, bg, wu, bu, wd, bd, n_chunks):
    def wspec(shape):
        return pl.BlockSpec(shape, lambda c, ce, na: (ce[c], 0, 0), pipeline_mode=pl.Buffered(1))

    def bspec():
        return pl.BlockSpec((1, 1, D_FF), lambda c, ce, na: (ce[c], 0, 0))

    grid_spec = pltpu.PrefetchScalarGridSpec(
        num_scalar_prefetch=2,
        grid=(n_chunks,),
        in_specs=[
            pl.BlockSpec((MOE_ROWS, D_MODEL), lambda c, ce, na: (c, 0)),
            wspec((1, D_MODEL, D_FF)), bspec(),
            wspec((1, D_MODEL, D_FF)), bspec(),
            wspec((1, D_FF, D_MODEL)), bspec(),
        ],
        out_specs=pl.BlockSpec((MOE_ROWS, D_MODEL), lambda c, ce, na: (c, 0)),
    )
    return pl.pallas_call(
        _experts_kernel,
        grid_spec=grid_spec,
        out_shape=jax.ShapeDtypeStruct((n_chunks * MOE_ROWS, D_MODEL), F32),
        compiler_params=_params(("arbitrary",), VMEM_LIMIT),
        name="experts",
    )(chunk_e, n_active, x_rows, wg, bg, wu, bu, wd, bd)


def _combine_kernel(dest_ref, gate_ref, x1_ref, gf_ref, y_hbm, o_ref, ybuf, sem):
    tb = TB_COMBINE
    n = tb * TOP_K

    def issue(r, carry):
        _row_copy(y_hbm, ybuf.at[r & (TOP_K - 1)], dest_ref[0, 0, r], r >> TOP_K_LOG2, sem.at[0]).start()
        return carry

    lax.fori_loop(0, n, issue, 0)

    def drain(r, carry):
        _row_copy(y_hbm, ybuf.at[0], 0, 0, sem.at[0]).wait()
        return carry

    lax.fori_loop(0, n, drain, 0)

    gate = gate_ref[...]
    out = x1_ref[...]
    for k in range(TOP_K):
        out = out + gate[:, k:k + 1] * ybuf[k]
    o_ref[...] = out * lax.rsqrt(jnp.mean(out * out, axis=-1, keepdims=True) + NORM_EPS) * gf_ref[...]


def _combine(dest, gate, x1, gf, y_rows):
    t = x1.shape[0]
    nt = t // TB_COMBINE
    return pl.pallas_call(
        _combine_kernel,
        grid=(nt,),
        in_specs=[
            pl.BlockSpec((1, 1, TB_COMBINE * TOP_K), lambda i: (i, 0, 0), memory_space=pltpu.SMEM),
            pl.BlockSpec((TB_COMBINE, LANES), lambda i: (i, 0)),
            pl.BlockSpec((TB_COMBINE, D_MODEL), lambda i: (i, 0)),
            pl.BlockSpec((1, D_MODEL), lambda i: (0, 0)),
            pl.BlockSpec(memory_space=pl.ANY),
        ],
        out_specs=pl.BlockSpec((TB_COMBINE, D_MODEL), lambda i: (i, 0)),
        out_shape=jax.ShapeDtypeStruct((t, D_MODEL), F32),
        scratch_shapes=[
            pltpu.VMEM((TOP_K, TB_COMBINE, D_MODEL), F32),
            pltpu.SemaphoreType.DMA((1,)),
        ],
        compiler_params=_params(("arbitrary",)),
        name="combine",
    )(dest.reshape(nt, 1, TB_COMBINE * TOP_K), gate, x1, gf, y_rows)


def _rope_tables(s):
    pos = jnp.arange(s, dtype=F32)
    inv = ROPE_THETA ** (-jnp.arange(0, HEAD_DIM, 2, dtype=F32) / HEAD_DIM)
    ang = pos[:, None] * inv[None, :]
    ang = jnp.concatenate([ang, ang], axis=-1)
    sign = jnp.where(jnp.arange(HEAD_DIM) < HEAD_DIM // 2, -1.0, 1.0).astype(F32)
    return jnp.cos(ang), jnp.sin(ang) * sign[None, :]


def _cmp_to_sel_t(nc, n_sel):
    cs = np.arange(nc) * CMP_STRIDE
    ss = np.arange(n_sel) * SEL_BLOCK
    ov = np.minimum(cs[:, None] + CMP_BLOCK, ss[None, :] + SEL_BLOCK) - np.maximum(cs[:, None], ss[None, :])
    m = np.clip(ov, 0, None) / CMP_BLOCK
    m[nc - 1, :] = 0.0
    out = np.zeros((LANES, nc), np.float32)
    out[:n_sel] = m.T
    return jnp.asarray(out)


def _block_expand(s):
    key = np.arange(s).reshape(s // TK_SEL, 1, TK_SEL)
    blk = np.arange(LANES).reshape(1, LANES, 1)
    return jnp.asarray((key // SEL_BLOCK == blk).astype(np.float32), dtype=BF16)


def kernel(x, norm1_g, w_in, cmp_pe_k, cmp_pe_v, cmp_k_w1, cmp_k_w2, cmp_v_w1, cmp_v_w2, lru_conv_w, lru_conv_b, lru_wa, lru_ba, lru_wi, lru_bi, lru_lambda, w_br_attn, w_br_lru, w_out, norm2_g, router_w, router_b, exp_w_gate, exp_b_gate, exp_w_up, exp_b_up, exp_w_down, exp_b_down, normf_g):
    b, s, d = x.shape
    assert d == D_MODEL and norm1_g.shape[0] == 1, "single-layer block of width D_MODEL"
    assert s % TM == 0 and s % TS_LRU == 0 and s >= WIN_KEYS and s // SEL_BLOCK <= LANES
    t = b * s
    x2 = x.reshape(t, d)

    w_in0 = w_in[0]
    o = 0
    wq = w_in0[:, o:o + Q_W].astype(BF16); o += Q_W
    wkv = w_in0[:, o:o + KV_KINDS * KV_W].astype(BF16); o += KV_KINDS * KV_W
    n_g = 3 * NSA_Q_HEADS
    wg_nsa = w_in0[:, o:o + n_g].reshape(d, NSA_KV_GROUPS, 3 * NSA_GROUP); o += n_g
    wg_nsa = jnp.pad(wg_nsa, ((0, 0), (0, 0), (0, LANES - 3 * NSA_GROUP))).reshape(d, NSA_KV_GROUPS * LANES).astype(BF16)
    wlxy = w_in0[:, o:o + 2 * LRU_WIDTH].astype(BF16); o += 2 * LRU_WIDTH
    wga = w_in0[:, o:o + D_MODEL].astype(BF16); o += D_MODEL
    wgb = w_in0[:, o:o + D_MODEL].astype(BF16); o += D_MODEL

    cos, sin_signed = _rope_tables(s)
    nc = s // CMP_STRIDE
    n_sel = s // SEL_BLOCK

    h, gates = _norm1(x2, norm1_g, wg_nsa)
    q_raw, q_rot = _qproj(h, wq, cos, sin_signed, s)
    kv6 = _kvproj(h, wkv, cos, sin_signed, b, s)
    lxy = _proj(h, wlxy, "lruproj")

    cw1 = jnp.stack([cmp_k_w1[0], cmp_v_w1[0]]).astype(BF16)
    cw2 = jnp.stack([cmp_k_w2[0], cmp_v_w2[0]]).astype(BF16)
    pe = jnp.stack([cmp_pe_k[0], cmp_pe_v[0]]).reshape(2, 1, CMP_BLOCK * HEAD_DIM)
    pe = jnp.pad(pe, ((0, 0), (0, SUBLANES - 1), (0, 0))).astype(BF16)
    kvc = _cmpkv(kv6, cw1, cw2, pe, b, s)
    ocmp, msk = _cmpattn(q_raw, kvc, _cmp_to_sel_t(nc, n_sel), b, s)
    o_a = _selwin(q_rot, kv6, msk, _block_expand(s), ocmp, gates, b, s)

    o_b = _lru(lxy, lru_conv_w[0], lru_conv_b, lru_wa[0].astype(BF16), lru_ba[0].reshape(1, LRU_WIDTH),
               lru_wi[0].astype(BF16), lru_bi[0].reshape(1, LRU_WIDTH), lru_lambda, b, s)

    merged = _merge(h, o_a.reshape(t, Q_W), o_b.reshape(t, LRU_WIDTH), wga, wgb,
                    w_br_attn[0].astype(BF16), w_br_lru[0].astype(BF16))

    rw = jnp.pad(router_w[0], ((0, 0), (0, LANES - N_EXPERTS)))
    rb = jnp.pad(router_b, ((0, 0), (0, LANES - N_EXPERTS)))
    tri = jnp.asarray(np.tril(np.ones((TM_OUT, TM_OUT), np.float32), -1), dtype=BF16)
    x1, h2, tope, gate, rank, cnt = _outrouter(merged, x2, w_out[0].astype(BF16), norm2_g, rw, rb, tri)

    counts = cnt[0, :N_EXPERTS]
    padded = (counts + MOE_ROWS - 1) // MOE_ROWS * MOE_ROWS
    pend = jnp.cumsum(padded)
    pstart = pend - padded
    n_rows = t * TOP_K + N_EXPERTS * MOE_ROWS
    n_chunks = n_rows // MOE_ROWS
    dest = (pstart[tope[:, :TOP_K]] + rank[:, :TOP_K]).astype(I32)
    tok_of = jnp.broadcast_to(jnp.arange(t, dtype=I32)[:, None], (t, TOP_K))
    row_tok = jnp.zeros((n_rows,), I32).at[dest.reshape(-1)].set(tok_of.reshape(-1))
    chunk_e = jnp.minimum(jnp.sum((jnp.arange(n_chunks) * MOE_ROWS)[:, None] >= pend[None, :], axis=1),
                          N_EXPERTS - 1).astype(I32)
    n_active = (pend[-1:] // MOE_ROWS).astype(I32)

    x_rows = _moegather(n_active, row_tok, h2, n_chunks)
    y_rows = _experts(chunk_e, n_active, x_rows,
                      exp_w_gate[0].astype(BF16), exp_b_gate[0].reshape(N_EXPERTS, 1, D_FF),
                      exp_w_up[0].astype(BF16), exp_b_up[0].reshape(N_EXPERTS, 1, D_FF),
                      exp_w_down[0].astype(BF16), exp_b_down[0].reshape(N_EXPERTS, 1, D_MODEL), n_chunks)
    out = _combine(dest, gate, x1, normf_g.reshape(1, D_MODEL), y_rows)
    return out.reshape(b, s, d)
```

```python
import functools

import numpy as np
import jax
import jax.numpy as jnp
from jax import lax
from jax.experimental import pallas as pl
from jax.experimental.pallas import tpu as pltpu

F32 = jnp.float32
BF16 = jnp.bfloat16
I32 = jnp.int32

D_MODEL = 2048
HEAD_DIM = 128
NSA_Q_HEADS = 16
NSA_KV_GROUPS = 4
NSA_GROUP = NSA_Q_HEADS // NSA_KV_GROUPS
CMP_BLOCK = 32
CMP_STRIDE = 16
CMP_HIDDEN = 256
SEL_BLOCK = 64
SEL_BLOCK_LOG2 = 6
SEL_TOPK = 16
WINDOW = 512
ROPE_THETA = 10000.0
LRU_WIDTH = 2048
LRU_BLOCKS = 16
LRU_BLOCK_W = LRU_WIDTH // LRU_BLOCKS
CONV_WIDTH = 4
LRU_C = 8.0
N_EXPERTS = 32
TOP_K = 4
TOP_K_LOG2 = 2
D_FF = 2048
SWIGLU_ALPHA = 1.702
SWIGLU_LIMIT = 7.0
NORM_EPS = 1e-6
NEG_INF = -1e30
FORCE_SCORE = 1e6

Q_W = NSA_Q_HEADS * HEAD_DIM
KV_W = NSA_KV_GROUPS * HEAD_DIM
GROUP_W = NSA_GROUP * HEAD_DIM
SCALE = HEAD_DIM ** -0.5
LANES = 128
SUBLANES = 8

TM = 512
TN = 512
TQ_CMP = 256
TQ = 256
TQ_LOG2 = 8
TK_SEL = 512
WIN_KEYS = WINDOW + TQ
TS_LRU = 256
CW_LRU = 512
TM_OUT = 256
MOE_ROWS = 256
FF_CHUNK = 512
TB_COMBINE = 128
DMA_UNROLL = 8
VMEM_LIMIT = 56 * 1024 * 1024


def _params(sem, vmem=None):
    return pltpu.CompilerParams(dimension_semantics=sem, vmem_limit_bytes=vmem)


def _dot(a, b):
    return jnp.dot(a, b, preferred_element_type=F32)


def _dot_nt(a, b, precision=None):
    return lax.dot_general(a, b, (((1,), (1,)), ((), ())), precision=precision, preferred_element_type=F32)


def _norm1_kernel(x_ref, g_ref, wg_ref, h_ref, gate_ref):
    x = x_ref[...]
    y = x * lax.rsqrt(jnp.mean(x * x, axis=-1, keepdims=True) + NORM_EPS) * g_ref[...]
    hb = y.astype(BF16)
    h_ref[...] = hb
    gate_ref[...] = jax.nn.sigmoid(_dot(hb, wg_ref[...]))


def _norm1(x2, g, wg):
    t = x2.shape[0]
    return pl.pallas_call(
        _norm1_kernel,
        grid=(t // TM,),
        in_specs=[
            pl.BlockSpec((TM, D_MODEL), lambda i: (i, 0)),
            pl.BlockSpec((1, D_MODEL), lambda i: (0, 0)),
            pl.BlockSpec((D_MODEL, NSA_KV_GROUPS * LANES), lambda i: (0, 0)),
        ],
        out_specs=[
            pl.BlockSpec((TM, D_MODEL), lambda i: (i, 0)),
            pl.BlockSpec((TM, NSA_KV_GROUPS * LANES), lambda i: (i, 0)),
        ],
        out_shape=[
            jax.ShapeDtypeStruct((t, D_MODEL), BF16),
            jax.ShapeDtypeStruct((t, NSA_KV_GROUPS * LANES), F32),
        ],
        compiler_params=_params(("parallel",)),
        name="norm1",
    )(x2, g, wg)


def _rope(x, cos, sin_signed):
    return x * cos + pltpu.roll(x, HEAD_DIM // 2, axis=1) * sin_signed


def _qproj_kernel(h_ref, w_ref, cos_ref, sin_ref, qraw_ref, qrot_ref):
    acc = _dot(h_ref[...], w_ref[...])
    qraw_ref[...] = acc.astype(BF16)
    cos = cos_ref[...]
    sin = sin_ref[...]
    for hh in range(TN // HEAD_DIM):
        sl = slice(hh * HEAD_DIM, (hh + 1) * HEAD_DIM)
        qrot_ref[:, sl] = _rope(acc[:, sl], cos, sin).astype(BF16)


def _qproj(h, wq, cos, sin, s):
    t = h.shape[0]
    ns = s // TM
    return pl.pallas_call(
        _qproj_kernel,
        grid=(Q_W // TN, t // TM),
        in_specs=[
            pl.BlockSpec((TM, D_MODEL), lambda j, i: (i, 0)),
            pl.BlockSpec((D_MODEL, TN), lambda j, i: (0, j)),
            pl.BlockSpec((TM, HEAD_DIM), lambda j, i: (i % ns, 0)),
            pl.BlockSpec((TM, HEAD_DIM), lambda j, i: (i % ns, 0)),
        ],
        out_specs=[
            pl.BlockSpec((TM, TN), lambda j, i: (i, j)),
            pl.BlockSpec((TM, TN), lambda j, i: (i, j)),
        ],
        out_shape=[jax.ShapeDtypeStruct((t, Q_W), BF16), jax.ShapeDtypeStruct((t, Q_W), BF16)],
        compiler_params=_params(("parallel", "parallel")),
        name="qproj",
    )(h, wq, cos, sin)


KV_KINDS = 6
KIND_K_SEL = 2
KIND_K_WIN = 4


def _kvproj_kernel(h_ref, w_ref, cos_ref, sin_ref, out_ref):
    kind = pl.program_id(0)
    acc = _dot(h_ref[...], w_ref[...])
    cos = cos_ref[...]
    sin = sin_ref[...]
    is_rope = jnp.logical_or(kind == KIND_K_SEL, kind == KIND_K_WIN)
    for g in range(NSA_KV_GROUPS):
        xh = acc[:, g * HEAD_DIM:(g + 1) * HEAD_DIM]
        out_ref[0, 0, g] = jnp.where(is_rope, _rope(xh, cos, sin), xh).astype(BF16)


def _kvproj(h, wkv, cos, sin, b, s):
    t = h.shape[0]
    ns = s // TM
    return pl.pallas_call(
        _kvproj_kernel,
        grid=(KV_KINDS, t // TM),
        in_specs=[
            pl.BlockSpec((TM, D_MODEL), lambda k, i: (i, 0)),
            pl.BlockSpec((D_MODEL, KV_W), lambda k, i: (0, k)),
            pl.BlockSpec((TM, HEAD_DIM), lambda k, i: (i % ns, 0)),
            pl.BlockSpec((TM, HEAD_DIM), lambda k, i: (i % ns, 0)),
        ],
        out_specs=pl.BlockSpec((1, 1, NSA_KV_GROUPS, TM, HEAD_DIM), lambda k, i: (k, i // ns, 0, i % ns, 0)),
        out_shape=jax.ShapeDtypeStruct((KV_KINDS, b, NSA_KV_GROUPS, s, HEAD_DIM), BF16),
        compiler_params=_params(("parallel", "parallel")),
        name="kvproj",
    )(h, wkv, cos, sin)


def _proj_kernel(h_ref, w_ref, o_ref):
    o_ref[...] = _dot(h_ref[...], w_ref[...]).astype(o_ref.dtype)


def _proj(h, w, name):
    t = h.shape[0]
    n = w.shape[1]
    return pl.pallas_call(
        _proj_kernel,
        grid=(n // TN, t // TM),
        in_specs=[
            pl.BlockSpec((TM, D_MODEL), lambda j, i: (i, 0)),
            pl.BlockSpec((D_MODEL, TN), lambda j, i: (0, j)),
        ],
        out_specs=pl.BlockSpec((TM, TN), lambda j, i: (i, j)),
        out_shape=jax.ShapeDtypeStruct((t, n), BF16),
        compiler_params=_params(("parallel", "parallel")),
        name=name,
    )(h, w)


def _cmpkv_kernel(c_ref, w1_ref, w2_ref, pe_ref, o_ref):
    c = c_ref[0, 0, 0]
    half = CMP_STRIDE * HEAD_DIM
    top = _dot(c, w1_ref[0, :half, :])
    bot = _dot(c, w1_ref[0, half:, :])
    n_rows = top.shape[0]
    bot_next = pltpu.roll(bot, n_rows - 1, axis=0)
    bias = _dot(pe_ref[0], w1_ref[0])[0:1, :]
    hid = jax.nn.gelu(top + bot_next + bias)
    o_ref[0, 0, 0] = _dot(hid.astype(BF16), w2_ref[0]).astype(BF16)


def _cmpkv(kv6, w1, w2, pe, b, s):
    nc = s // CMP_STRIDE
    kvc = kv6[:2].reshape(2, b, NSA_KV_GROUPS, nc, CMP_STRIDE * HEAD_DIM)
    return pl.pallas_call(
        _cmpkv_kernel,
        grid=(2, b, NSA_KV_GROUPS),
        in_specs=[
            pl.BlockSpec((1, 1, 1, nc, CMP_STRIDE * HEAD_DIM), lambda k, bi, g: (k, bi, g, 0, 0)),
            pl.BlockSpec((1, CMP_BLOCK * HEAD_DIM, CMP_HIDDEN), lambda k, bi, g: (k, 0, 0)),
            pl.BlockSpec((1, CMP_HIDDEN, HEAD_DIM), lambda k, bi, g: (k, 0, 0)),
            pl.BlockSpec((1, SUBLANES, CMP_BLOCK * HEAD_DIM), lambda k, bi, g: (k, 0, 0)),
        ],
        out_specs=pl.BlockSpec((1, 1, 1, nc, HEAD_DIM), lambda k, bi, g: (k, bi, g, 0, 0)),
        out_shape=jax.ShapeDtypeStruct((2, b, NSA_KV_GROUPS, nc, HEAD_DIM), BF16),
        compiler_params=_params(("parallel", "parallel", "parallel")),
        name="cmpkv",
    )(kvc, w1, w2, pe)


def _cmpattn_kernel(q_ref, kc_ref, vc_ref, mt_ref, ocmp_ref, msk_ref, *, n_sel):
    i = pl.program_id(2)
    tq = TQ_CMP
    q = q_ref[0]
    kc = kc_ref[0, 0, 0]
    vc = vc_ref[0, 0, 0]
    nc = kc.shape[0]
    t_row = i * tq + lax.broadcasted_iota(I32, (tq, 1), 0)
    n_lane = lax.broadcasted_iota(I32, (tq, nc), 1)
    cmask = (n_lane * CMP_STRIDE + (CMP_BLOCK - 1)) <= t_row
    psum = jnp.zeros((tq, nc), F32)
    for hh in range(NSA_GROUP):
        sl = slice(hh * HEAD_DIM, (hh + 1) * HEAD_DIM)
        s = _dot_nt(q[:, sl], kc) * SCALE
        s = jnp.where(cmask, s, NEG_INF)
        e = jnp.exp(s - jnp.max(s, axis=-1, keepdims=True))
        p = e / jnp.sum(e, axis=-1, keepdims=True)
        p = jnp.where(cmask, p, 0.0)
        ocmp_ref[0, :, sl] = _dot(p.astype(BF16), vc).astype(BF16)
        psum = psum + p
    imp_t = _dot_nt(mt_ref[...], psum, precision=lax.Precision.HIGHEST)
    imp_t = imp_t[:n_sel]
    jb = lax.broadcasted_iota(I32, (n_sel, tq), 0)
    cur = (i * tq + lax.broadcasted_iota(I32, (n_sel, tq), 1)) >> SEL_BLOCK_LOG2
    forced = (jb == 0) | (jb == cur) | (jb == cur - 1)
    valid = jb <= cur
    score = jnp.where(forced, FORCE_SCORE, jnp.where(valid, imp_t, NEG_INF))
    pad = jnp.zeros((LANES - n_sel, tq), F32)
    needs_rank = (i + 1) * tq > SEL_TOPK * SEL_BLOCK

    @pl.when(needs_rank)
    def _():
        rank = jnp.zeros((n_sel, tq), I32)
        for jp in range(n_sel):
            row = score[jp:jp + 1, :]
            beats = (row > score) | ((row == score) & (jb > jp))
            rank = rank + beats.astype(I32)
        member = (valid & (rank < SEL_TOPK)).astype(F32)
        msk_ref[0, 0] = jnp.concatenate([member, pad], axis=0).astype(BF16)

    @pl.when(jnp.logical_not(needs_rank))
    def _():
        msk_ref[0, 0] = jnp.concatenate([valid.astype(F32), pad], axis=0).astype(BF16)


def _cmpattn(q_raw, kvc, mt, b, s):
    nc = s // CMP_STRIDE
    n_sel = s // SEL_BLOCK
    q3 = q_raw.reshape(b, s, Q_W)
    return pl.pallas_call(
        functools.partial(_cmpattn_kernel, n_sel=n_sel),
        grid=(b, NSA_KV_GROUPS, s // TQ_CMP),
        in_specs=[
            pl.BlockSpec((1, TQ_CMP, GROUP_W), lambda bi, g, i: (bi, i, g)),
            pl.BlockSpec((1, 1, 1, nc, HEAD_DIM), lambda bi, g, i: (0, bi, g, 0, 0)),
            pl.BlockSpec((1, 1, 1, nc, HEAD_DIM), lambda bi, g, i: (1, bi, g, 0, 0)),
            pl.BlockSpec((LANES, nc), lambda bi, g, i: (0, 0)),
        ],
        out_specs=[
            pl.BlockSpec((1, TQ_CMP, GROUP_W), lambda bi, g, i: (bi, i, g)),
            pl.BlockSpec((1, 1, LANES, TQ_CMP), lambda bi, g, i: (bi, g, 0, i)),
        ],
        out_shape=[
            jax.ShapeDtypeStruct((b, s, Q_W), BF16),
            jax.ShapeDtypeStruct((b, NSA_KV_GROUPS, LANES, s), BF16),
        ],
        compiler_params=_params(("parallel", "parallel", "parallel")),
        name="cmpattn",
    )(q3, kvc, kvc, mt)


def _selwin_kernel(q_ref, ks_ref, vs_ref, kw_ref, vw_ref, msk_ref, emat_ref, ocmp_ref, gate_ref, o_ref,
                   acc_sc, *, seq):
    i = pl.program_id(2)
    tq = TQ
    q = q_ref[0]
    heads = [slice(hh * tq, (hh + 1) * tq) for hh in range(NSA_GROUP)]
    q_t = jnp.concatenate(
        [q[:, hh * HEAD_DIM:(hh + 1) * HEAD_DIM].astype(F32).T.astype(BF16) for hh in range(NSA_GROUP)], axis=1)
    t_lane = i * tq + lax.broadcasted_iota(I32, (1, tq), 1)

    acc_sc[...] = jnp.zeros(acc_sc.shape, F32)
    selm_t = msk_ref[0, 0]
    n_kv = (i * tq + tq - 1) // TK_SEL + 1

    def sel_step(j, carry):
        m_old, l_old = carry
        off = pl.multiple_of(j * TK_SEL, TK_SEL)
        k = ks_ref[0, 0, 0, pl.ds(off, TK_SEL), :]
        s_t = _dot(k, q_t) * SCALE
        picked = _dot(emat_ref[j], selm_t)
        kpos = off + lax.broadcasted_iota(I32, (TK_SEL, tq), 0)
        ok = (picked > 0.5) & (kpos <= t_lane)
        m_new, l_new, alphas, ps = [], [], [], []
        for hh in range(NSA_GROUP):
            sh = jnp.where(ok, s_t[:, heads[hh]], NEG_INF)
            mh = jnp.maximum(m_old[hh], jnp.max(sh, axis=0, keepdims=True))
            alpha = jnp.exp(m_old[hh] - mh)
            p = jnp.exp(sh - mh)
            m_new.append(mh)
            l_new.append(alpha * l_old[hh] + jnp.sum(p, axis=0, keepdims=True))
            alphas.append(alpha)
            ps.append(p.astype(BF16))
        p_t = jnp.concatenate(ps, axis=1)
        acc_sc[...] = acc_sc[...] * jnp.concatenate(alphas, axis=1) + _dot(vs_ref[0, 0, j], p_t)
        return tuple(m_new), tuple(l_new)

    m0 = tuple(jnp.full((1, tq), NEG_INF, F32) for _ in range(NSA_GROUP))
    l0 = tuple(jnp.zeros((1, tq), F32) for _ in range(NSA_GROUP))
    _, l_sel = lax.fori_loop(0, n_kv, sel_step, (m0, l0))

    start = jnp.clip((i - WINDOW // tq) * tq, 0, seq - WIN_KEYS)
    start = pl.multiple_of(start, tq)
    kw = kw_ref[0, 0, 0, pl.ds(start, WIN_KEYS), :]
    sw_t = _dot(kw, q_t) * SCALE
    kp = start + lax.broadcasted_iota(I32, (WIN_KEYS, tq), 0)
    okw = (kp <= t_lane) & (kp > t_lane - WINDOW)
    es, inv_lw = [], []
    for hh in range(NSA_GROUP):
        swh = jnp.where(okw, sw_t[:, heads[hh]], NEG_INF)
        ew = jnp.exp(swh - jnp.max(swh, axis=0, keepdims=True))
        inv_lw.append(1.0 / jnp.sum(ew, axis=0, keepdims=True))
        es.append(ew.astype(BF16))
    e_t = jnp.concatenate(es, axis=1)
    c0 = start >> TQ_LOG2
    ow_t = jnp.zeros((HEAD_DIM, NSA_GROUP * tq), F32)
    for c in range(WIN_KEYS // tq):
        ow_t = ow_t + _dot(vw_ref[0, 0, c0 + c], e_t[c * tq:(c + 1) * tq, :])

    gate = gate_ref[0]
    gate_t = gate.T
    ocmp = ocmp_ref[0]
    for hh in range(NSA_GROUP):
        cols = slice(hh * HEAD_DIM, (hh + 1) * HEAD_DIM)
        ls = l_sel[hh]
        o_sel_t = acc_sc[:, heads[hh]] * jnp.where(ls > 0.0, 1.0 / ls, 0.0)
        o_win_t = ow_t[:, heads[hh]] * inv_lw[hh]
        z_t = gate_t[3 * hh + 1:3 * hh + 2, :] * o_sel_t + gate_t[3 * hh + 2:3 * hh + 3, :] * o_win_t
        o = gate[:, 3 * hh:3 * hh + 1] * ocmp[:, cols].astype(F32) + z_t.T
        o_ref[0, :, cols] = o.astype(BF16)


def _selwin(q_rot, kv6, msk, emat, ocmp, gates, b, s):
    q3 = q_rot.reshape(b, s, Q_W)
    g3 = gates.reshape(b, s, NSA_KV_GROUPS * LANES)

    def k_spec(kind):
        return pl.BlockSpec((1, 1, 1, s, HEAD_DIM), lambda bi, g, i: (kind, bi, g, 0, 0))

    def vt_tiles(kind, tile):
        return jnp.swapaxes(kv6[kind].reshape(b, NSA_KV_GROUPS, s // tile, tile, HEAD_DIM), -1, -2)

    def vt_spec(tile):
        return pl.BlockSpec((1, 1, s // tile, HEAD_DIM, tile), lambda bi, g, i: (bi, g, 0, 0, 0))

    return pl.pallas_call(
        functools.partial(_selwin_kernel, seq=s),
        grid=(b, NSA_KV_GROUPS, s // TQ),
        in_specs=[
            pl.BlockSpec((1, TQ, GROUP_W), lambda bi, g, i: (bi, i, g)),
            k_spec(2), vt_spec(TK_SEL), k_spec(4), vt_spec(TQ),
            pl.BlockSpec((1, 1, LANES, TQ), lambda bi, g, i: (bi, g, 0, i)),
            pl.BlockSpec((s // TK_SEL, TK_SEL, LANES), lambda bi, g, i: (0, 0, 0)),
            pl.BlockSpec((1, TQ, GROUP_W), lambda bi, g, i: (bi, i, g)),
            pl.BlockSpec((1, TQ, LANES), lambda bi, g, i: (bi, i, g)),
        ],
        out_specs=pl.BlockSpec((1, TQ, GROUP_W), lambda bi, g, i: (bi, i, g)),
        out_shape=jax.ShapeDtypeStruct((b, s, Q_W), BF16),
        scratch_shapes=[pltpu.VMEM((HEAD_DIM, NSA_GROUP * TQ), F32)],
        compiler_params=_params(("parallel", "parallel", "parallel")),
        name="selwin",
    )(q3, kv6, vt_tiles(3, TK_SEL), kv6, vt_tiles(5, TQ), msk, emat, ocmp, g3)


def _softplus(x):
    return jnp.maximum(x, 0.0) + jnp.log1p(jnp.exp(-jnp.abs(x)))


def _lru_kernel(lx_ref, ly_ref, cw_ref, cb_ref, wa_ref, ba_ref, wi_ref, bi_ref, lam_ref, o_ref, xe_sc, h_sc):
    ts = TS_LRU
    halo = SUBLANES

    @pl.when(pl.program_id(2) == 0)
    def _():
        xe_sc[0:halo, :] = jnp.zeros((halo, CW_LRU), F32)
        h_sc[...] = jnp.zeros(h_sc.shape, F32)

    x = lx_ref[0].astype(F32)
    xe_sc[halo:halo + ts, :] = x
    xc = cb_ref[...]
    for k in range(CONV_WIDTH):
        d = CONV_WIDTH - 1 - k
        xc = xc + xe_sc[pl.ds(halo - d, ts), :] * cw_ref[k:k + 1, :]
    xe_sc[0:halo, :] = x[ts - halo:ts]

    row_in_group = lax.broadcasted_iota(I32, (ts, LRU_BLOCK_W), 0) & (SUBLANES - 1)
    for n in range(CW_LRU // LRU_BLOCK_W):
        cols = slice(n * LRU_BLOCK_W, (n + 1) * LRU_BLOCK_W)
        xcn = xc[:, cols]
        xb = xcn.astype(BF16)
        r = jax.nn.sigmoid(_dot(xb, wa_ref[n]) + ba_ref[:, cols])
        ig = jax.nn.sigmoid(_dot(xb, wi_ref[n]) + bi_ref[:, cols])
        log_a = (-LRU_C * _softplus(-lam_ref[:, cols])) * r
        a = jnp.exp(log_a)
        u = jnp.sqrt(-jnp.tanh(log_a) * (a * a + 1.0)) * (ig * xcn)
        ca, cb = a, u
        for d in (1, 2, 4):
            ok = row_in_group >= d
            ca_sh = pltpu.roll(ca, d, axis=0)
            cb_sh = pltpu.roll(cb, d, axis=0)
            cb = jnp.where(ok, ca * cb_sh + cb, cb)
            ca = jnp.where(ok, ca * ca_sh, ca)
        hprev = h_sc[:, cols]
        hs = []
        for gi in range(ts // SUBLANES):
            rows = slice(gi * SUBLANES, (gi + 1) * SUBLANES)
            hg = ca[rows] * hprev + cb[rows]
            hs.append(hg)
            hprev = hg[SUBLANES - 1:SUBLANES]
        h_sc[:, cols] = hprev
        h = jnp.concatenate(hs, axis=0)
        o_ref[0, :, cols] = (h * jax.nn.gelu(ly_ref[0, :, cols].astype(F32))).astype(BF16)


def _lru(lxy, conv_w, conv_b, wa, ba, wi, bi, lam, b, s):
    lxy3 = lxy.reshape(b, s, 2 * LRU_WIDTH)
    ncw = LRU_WIDTH // CW_LRU
    nb = CW_LRU // LRU_BLOCK_W
    vec = pl.BlockSpec((1, CW_LRU), lambda bi_, c, t: (0, c))
    return pl.pallas_call(
        _lru_kernel,
        grid=(b, ncw, s // TS_LRU),
        in_specs=[
            pl.BlockSpec((1, TS_LRU, CW_LRU), lambda bi_, c, t: (bi_, t, c)),
            pl.BlockSpec((1, TS_LRU, CW_LRU), lambda bi_, c, t: (bi_, t, ncw + c)),
            pl.BlockSpec((CONV_WIDTH, CW_LRU), lambda bi_, c, t: (0, c)),
            vec,
            pl.BlockSpec((nb, LRU_BLOCK_W, LRU_BLOCK_W), lambda bi_, c, t: (c, 0, 0)),
            vec,
            pl.BlockSpec((nb, LRU_BLOCK_W, LRU_BLOCK_W), lambda bi_, c, t: (c, 0, 0)),
            vec,
            vec,
        ],
        out_specs=pl.BlockSpec((1, TS_LRU, CW_LRU), lambda bi_, c, t: (bi_, t, c)),
        out_shape=jax.ShapeDtypeStruct((b, s, LRU_WIDTH), BF16),
        scratch_shapes=[
            pltpu.VMEM((TS_LRU + SUBLANES, CW_LRU), F32),
            pltpu.VMEM((1, CW_LRU), F32),
        ],
        compiler_params=_params(("parallel", "parallel", "arbitrary")),
        name="lru",
    )(lxy3, lxy3, conv_w, conv_b, wa, ba, wi, bi, lam)


def _merge_kernel(h_ref, oa_ref, ob_ref, wga_ref, wgb_ref, wpa_ref, wpb_ref, o_ref):
    h = h_ref[...]
    ga = jax.nn.sigmoid(_dot(h, wga_ref[...]))
    gb = jax.nn.sigmoid(_dot(h, wgb_ref[...]))
    o_ref[...] = (ga * _dot(oa_ref[...], wpa_ref[...]) + gb * _dot(ob_ref[...], wpb_ref[...])).astype(BF16)


def _merge(h, oa, ob, wga, wgb, wpa, wpb):
    t = h.shape[0]
    row = pl.BlockSpec((TM, D_MODEL), lambda j, i: (i, 0))
    col = pl.BlockSpec((D_MODEL, TN), lambda j, i: (0, j))
    return pl.pallas_call(
        _merge_kernel,
        grid=(D_MODEL // TN, t // TM),
        in_specs=[row, row, row, col, col, col, col],
        out_specs=pl.BlockSpec((TM, TN), lambda j, i: (i, j)),
        out_shape=jax.ShapeDtypeStruct((t, D_MODEL), BF16),
        compiler_params=_params(("parallel", "parallel"), VMEM_LIMIT),
        name="merge",
    )(h, oa, ob, wga, wgb, wpa, wpb)


def _outrouter_kernel(mg_ref, x_ref, wo_ref, g2_ref, rw_ref, rb_ref, tri_ref,
                      x1_ref, h2_ref, tope_ref, gate_ref, rank_ref, cnt_ref, carry_sc):
    tm = TM_OUT

    @pl.when(pl.program_id(0) == 0)
    def _():
        carry_sc[...] = jnp.zeros(carry_sc.shape, F32)

    x1 = x_ref[...] + _dot(mg_ref[...], wo_ref[...])
    x1_ref[...] = x1
    y = x1 * lax.rsqrt(jnp.mean(x1 * x1, axis=-1, keepdims=True) + NORM_EPS) * g2_ref[...]
    h2_ref[...] = y
    logits = jnp.dot(y, rw_ref[...], precision=lax.Precision.HIGHEST, preferred_element_type=F32) + rb_ref[...]
    lane = lax.broadcasted_iota(I32, (tm, LANES), 1)
    lane_f = lane.astype(F32)
    work = jnp.where(lane < N_EXPERTS, logits, -jnp.inf)
    vals, idxs = [], []
    for _k in range(TOP_K):
        m = jnp.max(work, axis=-1, keepdims=True)
        idx = jnp.min(jnp.where(work == m, lane_f, float(LANES)), axis=-1, keepdims=True).astype(I32)
        vals.append(m)
        idxs.append(idx)
        work = jnp.where(lane == idx, -jnp.inf, work)
    es = [jnp.exp(v - vals[0]) for v in vals]
    denom = es[0] + es[1] + es[2] + es[3]
    cnt = jnp.zeros((tm, LANES), F32)
    for idx in idxs:
        cnt = cnt + (lane == idx).astype(F32)
    before = _dot(tri_ref[...], cnt.astype(BF16)) + carry_sc[...]
    tope = jnp.zeros((tm, LANES), I32)
    gate = jnp.zeros((tm, LANES), F32)
    rank = jnp.zeros((tm, LANES), F32)
    for k in range(TOP_K):
        tope = jnp.where(lane == k, idxs[k], tope)
        gate = jnp.where(lane == k, es[k] / denom, gate)
        rk = jnp.sum(jnp.where(lane == idxs[k], before, 0.0), axis=-1, keepdims=True)
        rank = jnp.where(lane == k, rk, rank)
    tope_ref[...] = tope
    gate_ref[...] = gate
    rank_ref[...] = rank.astype(I32)
    carry = carry_sc[...] + jnp.sum(cnt, axis=0, keepdims=True)
    carry_sc[...] = carry
    cnt_ref[...] = jnp.broadcast_to(carry, cnt_ref.shape).astype(I32)


def _outrouter(merged, x2, wo, g2, rw, rb, tri):
    t = x2.shape[0]
    row_bf = pl.BlockSpec((TM_OUT, D_MODEL), lambda i: (i, 0))
    lane_blk = pl.BlockSpec((TM_OUT, LANES), lambda i: (i, 0))
    return pl.pallas_call(
        _outrouter_kernel,
        grid=(t // TM_OUT,),
        in_specs=[
            row_bf,
            row_bf,
            pl.BlockSpec((D_MODEL, D_MODEL), lambda i: (0, 0)),
            pl.BlockSpec((1, D_MODEL), lambda i: (0, 0)),
            pl.BlockSpec((D_MODEL, LANES), lambda i: (0, 0)),
            pl.BlockSpec((1, LANES), lambda i: (0, 0)),
            pl.BlockSpec((TM_OUT, TM_OUT), lambda i: (0, 0)),
        ],
        out_specs=[row_bf, row_bf, lane_blk, lane_blk, lane_blk,
                   pl.BlockSpec((SUBLANES, LANES), lambda i: (0, 0))],
        out_shape=[
            jax.ShapeDtypeStruct((t, D_MODEL), F32),
            jax.ShapeDtypeStruct((t, D_MODEL), F32),
            jax.ShapeDtypeStruct((t, LANES), I32),
            jax.ShapeDtypeStruct((t, LANES), F32),
            jax.ShapeDtypeStruct((t, LANES), I32),
            jax.ShapeDtypeStruct((SUBLANES, LANES), I32),
        ],
        scratch_shapes=[pltpu.VMEM((1, LANES), F32)],
        compiler_params=_params(("arbitrary",), VMEM_LIMIT),
        name="outrouter",
    )(merged, x2, wo, g2, rw, rb, tri)


def _row_copy(src_hbm, dst, src_row, dst_row, sem):
    return pltpu.make_async_copy(src_hbm.at[pl.ds(src_row, 1)], dst.at[pl.ds(dst_row, 1)], sem)


def _issue_rows(n_rows, copy_of_row):
    def body(r, carry):
        copy_of_row(r).start()
        return carry

    lax.fori_loop(0, n_rows, body, 0, unroll=DMA_UNROLL)


def _drain_rows(n_rows, any_row_copy):
    def body(r, carry):
        any_row_copy.wait()
        return carry

    lax.fori_loop(0, n_rows, body, 0, unroll=DMA_UNROLL)


def _experts_kernel(ce_ref, na_ref, tokc_ref, tokn_ref, h_hbm, wg_ref, bg_ref, wu_ref, bu_ref, wd_ref, bd_ref,
                    y_ref, xbuf, sem):
    c = pl.program_id(0)
    n_act = na_ref[0]
    slot = c & 1

    def gather(tok_ref, s_):
        _issue_rows(MOE_ROWS, lambda r: _row_copy(h_hbm, xbuf.at[s_], tok_ref[0, 0, r], r, sem.at[s_]))

    @pl.when(jnp.logical_and(c == 0, n_act > 0))
    def _():
        gather(tokc_ref, 0)

    @pl.when(c + 1 < n_act)
    def _():
        gather(tokn_ref, 1 - slot)

    @pl.when(c < n_act)
    def _():
        _drain_rows(MOE_ROWS, _row_copy(h_hbm, xbuf.at[slot], 0, 0, sem.at[slot]))
        x = xbuf[slot].astype(BF16)
        acc = jnp.zeros((MOE_ROWS, D_MODEL), F32)
        for f in range(D_FF // FF_CHUNK):
            cols = slice(f * FF_CHUNK, (f + 1) * FF_CHUNK)
            g = _dot(x, wg_ref[0, :, cols]) + bg_ref[0, :, cols]
            u = _dot(x, wu_ref[0, :, cols]) + bu_ref[0, :, cols]
            g = jnp.minimum(g, SWIGLU_LIMIT)
            u = jnp.clip(u, -SWIGLU_LIMIT, SWIGLU_LIMIT)
            act = (u + 1.0) * (g * jax.nn.sigmoid(SWIGLU_ALPHA * g))
            acc = acc + _dot(act.astype(BF16), wd_ref[0, cols, :])
        y_ref[...] = acc + bd_ref[0]

    @pl.when(c >= n_act)
    def _():
        y_ref[...] = jnp.zeros(y_ref.shape, y_ref.dtype)


def _experts(chunk_e, n_active, row_tok, h2, wg, bg, wu, bu, wd, bd, n_chunks):
    def wspec(shape):
        return pl.BlockSpec(shape, lambda c, ce, na: (ce[c], 0, 0), pipeline_mode=pl.Buffered(1))

    def bspec():
        return pl.BlockSpec((1, 1, D_FF), lambda c, ce, na: (ce[c], 0, 0))

    tok3 = row_tok.reshape(n_chunks, 1, MOE_ROWS)
    grid_spec = pltpu.PrefetchScalarGridSpec(
        num_scalar_prefetch=2,
        grid=(n_chunks,),
        in_specs=[
            pl.BlockSpec((1, 1, MOE_ROWS), lambda c, ce, na: (c, 0, 0), memory_space=pltpu.SMEM),
            pl.BlockSpec((1, 1, MOE_ROWS), lambda c, ce, na: (jnp.minimum(c + 1, n_chunks - 1), 0, 0),
                         memory_space=pltpu.SMEM),
            pl.BlockSpec(memory_space=pl.ANY),
            wspec((1, D_MODEL, D_FF)), bspec(),
            wspec((1, D_MODEL, D_FF)), bspec(),
            wspec((1, D_FF, D_MODEL)), bspec(),
        ],
        out_specs=pl.BlockSpec((MOE_ROWS, D_MODEL), lambda c, ce, na: (c, 0)),
        scratch_shapes=[
            pltpu.VMEM((2, MOE_ROWS, D_MODEL), F32),
            pltpu.SemaphoreType.DMA((2,)),
        ],
    )
    return pl.pallas_call(
        _experts_kernel,
        grid_spec=grid_spec,
        out_shape=jax.ShapeDtypeStruct((n_chunks * MOE_ROWS, D_MODEL), F32),
        compiler_params=_params(("arbitrary",), VMEM_LIMIT),
        name="experts",
    )(chunk_e, n_active, tok3, tok3, h2, wg, bg, wu, bu, wd, bd)


def _combine_kernel(destc_ref, destn_ref, gate_ref, x1_ref, gf_ref, y_hbm, o_ref, ybuf, sem):
    i = pl.program_id(0)
    slot = i & 1
    n = TB_COMBINE * TOP_K

    def gather(dest_ref, s_):
        _issue_rows(n, lambda r: _row_copy(y_hbm, ybuf.at[s_, r & (TOP_K - 1)], dest_ref[0, 0, r],
                                           r >> TOP_K_LOG2, sem.at[s_]))

    @pl.when(i == 0)
    def _():
        gather(destc_ref, 0)

    @pl.when(i + 1 < pl.num_programs(0))
    def _():
        gather(destn_ref, 1 - slot)

    _drain_rows(n, _row_copy(y_hbm, ybuf.at[slot, 0], 0, 0, sem.at[slot]))
    gate = gate_ref[...]
    out = x1_ref[...]
    for k in range(TOP_K):
        out = out + gate[:, k:k + 1] * ybuf[slot, k]
    o_ref[...] = out * lax.rsqrt(jnp.mean(out * out, axis=-1, keepdims=True) + NORM_EPS) * gf_ref[...]


def _combine(dest, gate, x1, gf, y_rows):
    t = x1.shape[0]
    nt = t // TB_COMBINE
    dest3 = dest.reshape(nt, 1, TB_COMBINE * TOP_K)
    return pl.pallas_call(
        _combine_kernel,
        grid=(nt,),
        in_specs=[
            pl.BlockSpec((1, 1, TB_COMBINE * TOP_K), lambda i: (i, 0, 0), memory_space=pltpu.SMEM),
            pl.BlockSpec((1, 1, TB_COMBINE * TOP_K), lambda i: (jnp.minimum(i + 1, nt - 1), 0, 0),
                         memory_space=pltpu.SMEM),
            pl.BlockSpec((TB_COMBINE, LANES), lambda i: (i, 0)),
            pl.BlockSpec((TB_COMBINE, D_MODEL), lambda i: (i, 0)),
            pl.BlockSpec((1, D_MODEL), lambda i: (0, 0)),
            pl.BlockSpec(memory_space=pl.ANY),
        ],
        out_specs=pl.BlockSpec((TB_COMBINE, D_MODEL), lambda i: (i, 0)),
        out_shape=jax.ShapeDtypeStruct((t, D_MODEL), F32),
        scratch_shapes=[
            pltpu.VMEM((2, TOP_K, TB_COMBINE, D_MODEL), F32),
            pltpu.SemaphoreType.DMA((2,)),
        ],
        compiler_params=_params(("arbitrary",)),
        name="combine",
    )(dest3, dest3, gate, x1, gf, y_rows)


def _rope_tables(s):
    pos = jnp.arange(s, dtype=F32)
    inv = ROPE_THETA ** (-jnp.arange(0, HEAD_DIM, 2, dtype=F32) / HEAD_DIM)
    ang = pos[:, None] * inv[None, :]
    ang = jnp.concatenate([ang, ang], axis=-1)
    sign = jnp.where(jnp.arange(HEAD_DIM) < HEAD_DIM // 2, -1.0, 1.0).astype(F32)
    return jnp.cos(ang), jnp.sin(ang) * sign[None, :]


def _cmp_to_sel_t(nc, n_sel):
    cs = np.arange(nc) * CMP_STRIDE
    ss = np.arange(n_sel) * SEL_BLOCK
    ov = np.minimum(cs[:, None] + CMP_BLOCK, ss[None, :] + SEL_BLOCK) - np.maximum(cs[:, None], ss[None, :])
    m = np.clip(ov, 0, None) / CMP_BLOCK
    m[nc - 1, :] = 0.0
    out = np.zeros((LANES, nc), np.float32)
    out[:n_sel] = m.T
    return jnp.asarray(out)


def _block_expand(s):
    key = np.arange(s).reshape(s // TK_SEL, TK_SEL, 1)
    blk = np.arange(LANES).reshape(1, 1, LANES)
    return jnp.asarray((key // SEL_BLOCK == blk).astype(np.float32), dtype=BF16)


def kernel(x, norm1_g, w_in, cmp_pe_k, cmp_pe_v, cmp_k_w1, cmp_k_w2, cmp_v_w1, cmp_v_w2, lru_conv_w, lru_conv_b, lru_wa, lru_ba, lru_wi, lru_bi, lru_lambda, w_br_attn, w_br_lru, w_out, norm2_g, router_w, router_b, exp_w_gate, exp_b_gate, exp_w_up, exp_b_up, exp_w_down, exp_b_down, normf_g):
    b, s, d = x.shape
    assert d == D_MODEL and norm1_g.shape[0] == 1, "single-layer block of width D_MODEL"
    assert s % TM == 0 and s % TS_LRU == 0 and s >= WIN_KEYS and s // SEL_BLOCK <= LANES
    t = b * s
    x2 = x.reshape(t, d)

    w_in0 = w_in[0]
    o = 0
    wq = w_in0[:, o:o + Q_W].astype(BF16); o += Q_W
    wkv = w_in0[:, o:o + KV_KINDS * KV_W].astype(BF16); o += KV_KINDS * KV_W
    n_g = 3 * NSA_Q_HEADS
    wg_nsa = w_in0[:, o:o + n_g].reshape(d, NSA_KV_GROUPS, 3 * NSA_GROUP); o += n_g
    wg_nsa = jnp.pad(wg_nsa, ((0, 0), (0, 0), (0, LANES - 3 * NSA_GROUP))).reshape(d, NSA_KV_GROUPS * LANES).astype(BF16)
    wlxy = w_in0[:, o:o + 2 * LRU_WIDTH].astype(BF16); o += 2 * LRU_WIDTH
    wga = w_in0[:, o:o + D_MODEL].astype(BF16); o += D_MODEL
    wgb = w_in0[:, o:o + D_MODEL].astype(BF16); o += D_MODEL

    cos, sin_signed = _rope_tables(s)
    nc = s // CMP_STRIDE
    n_sel = s // SEL_BLOCK

    h, gates = _norm1(x2, norm1_g, wg_nsa)
    q_raw, q_rot = _qproj(h, wq, cos, sin_signed, s)
    kv6 = _kvproj(h, wkv, cos, sin_signed, b, s)
    lxy = _proj(h, wlxy, "lruproj")

    cw1 = jnp.stack([cmp_k_w1[0], cmp_v_w1[0]]).astype(BF16)
    cw2 = jnp.stack([cmp_k_w2[0], cmp_v_w2[0]]).astype(BF16)
    pe = jnp.stack([cmp_pe_k[0], cmp_pe_v[0]]).reshape(2, 1, CMP_BLOCK * HEAD_DIM)
    pe = jnp.pad(pe, ((0, 0), (0, SUBLANES - 1), (0, 0))).astype(BF16)
    kvc = _cmpkv(kv6, cw1, cw2, pe, b, s)
    ocmp, msk = _cmpattn(q_raw, kvc, _cmp_to_sel_t(nc, n_sel), b, s)
    o_a = _selwin(q_rot, kv6, msk, _block_expand(s), ocmp, gates, b, s)

    o_b = _lru(lxy, lru_conv_w[0], lru_conv_b, lru_wa[0].astype(BF16), lru_ba[0].reshape(1, LRU_WIDTH),
               lru_wi[0].astype(BF16), lru_bi[0].reshape(1, LRU_WIDTH), lru_lambda, b, s)

    merged = _merge(h, o_a.reshape(t, Q_W), o_b.reshape(t, LRU_WIDTH), wga, wgb,
                    w_br_attn[0].astype(BF16), w_br_lru[0].astype(BF16))

    rw = jnp.pad(router_w[0], ((0, 0), (0, LANES - N_EXPERTS)))
    rb = jnp.pad(router_b, ((0, 0), (0, LANES - N_EXPERTS)))
    tri = jnp.asarray(np.tril(np.ones((TM_OUT, TM_OUT), np.float32), -1), dtype=BF16)
    x1, h2, tope, gate, rank, cnt = _outrouter(merged, x2, w_out[0].astype(BF16), norm2_g, rw, rb, tri)

    counts = cnt[0, :N_EXPERTS]
    padded = (counts + MOE_ROWS - 1) // MOE_ROWS * MOE_ROWS
    pend = jnp.cumsum(padded)
    pstart = pend - padded
    n_rows = t * TOP_K + N_EXPERTS * MOE_ROWS
    n_chunks = n_rows // MOE_ROWS
    dest = (pstart[tope[:, :TOP_K]] + rank[:, :TOP_K]).astype(I32)
    tok_of = jnp.broadcast_to(jnp.arange(t, dtype=I32)[:, None], (t, TOP_K))
    row_tok = jnp.zeros((n_rows,), I32).at[dest.reshape(-1)].set(tok_of.reshape(-1))
    chunk_e = jnp.minimum(jnp.sum((jnp.arange(n_chunks) * MOE_ROWS)[:, None] >= pend[None, :], axis=1),
                          N_EXPERTS - 1).astype(I32)
    n_active = (pend[-1:] // MOE_ROWS).astype(I32)

    y_rows = _experts(chunk_e, n_active, row_tok, h2,
                      exp_w_gate[0].astype(BF16), exp_b_gate[0].reshape(N_EXPERTS, 1, D_FF),
                      exp_w_up[0].astype(BF16), exp_b_up[0].reshape(N_EXPERTS, 1, D_FF),
                      exp_w_down[0].astype(BF16), exp_b_down[0].reshape(N_EXPERTS, 1, D_MODEL), n_chunks)
    out = _combine(dest, gate, x1, normf_g.reshape(1, D_MODEL), y_rows)
    return out.reshape(b, s, d)
```

```python
import functools

import numpy as np
import jax
import jax.numpy as jnp
from jax import lax
from jax.experimental import pallas as pl
from jax.experimental.pallas import tpu as pltpu

F32 = jnp.float32
BF16 = jnp.bfloat16
I32 = jnp.int32

D_MODEL = 2048
HEAD_DIM = 128
NSA_Q_HEADS = 16
NSA_KV_GROUPS = 4
NSA_GROUP = NSA_Q_HEADS // NSA_KV_GROUPS
CMP_BLOCK = 32
CMP_STRIDE = 16
CMP_HIDDEN = 256
SEL_BLOCK = 64
SEL_BLOCK_LOG2 = 6
SEL_TOPK = 16
WINDOW = 512
ROPE_THETA = 10000.0
LRU_WIDTH = 2048
LRU_BLOCKS = 16
LRU_BLOCK_W = LRU_WIDTH // LRU_BLOCKS
CONV_WIDTH = 4
LRU_C = 8.0
N_EXPERTS = 32
TOP_K = 4
TOP_K_LOG2 = 2
D_FF = 2048
SWIGLU_ALPHA = 1.702
SWIGLU_LIMIT = 7.0
NORM_EPS = 1e-6
NEG_INF = -1e30
FORCE_SCORE = 1e6

Q_W = NSA_Q_HEADS * HEAD_DIM
KV_W = NSA_KV_GROUPS * HEAD_DIM
GROUP_W = NSA_GROUP * HEAD_DIM
SCALE = HEAD_DIM ** -0.5
LOG2_E = 1.4426950408889634
LANES = 128
SUBLANES = 8

TM = 1024
TN = 512
TQ_CMP = 256
TQ = 256
TQ_LOG2 = 8
TK_SEL = 256
WIN_KEYS = WINDOW + TQ
TS_LRU = 256
CW_LRU = 512
TM_OUT = 256
MOE_ROWS = 256
FF_CHUNK = 512
TB_COMBINE = 128
DMA_UNROLL = 8
W_STAGE = 256
N_WSLOTS = 2
VMEM_LIMIT = 56 * 1024 * 1024
VMEM_LIMIT_EXPERTS = 60 * 1024 * 1024


def _params(sem, vmem=None):
    return pltpu.CompilerParams(dimension_semantics=sem, vmem_limit_bytes=vmem)


def _dot(a, b):
    return jnp.dot(a, b, preferred_element_type=F32)


def _dot_nt(a, b, precision=None):
    return lax.dot_general(a, b, (((1,), (1,)), ((), ())), precision=precision, preferred_element_type=F32)


def _norm1_kernel(x_ref, g_ref, wg_ref, h_ref, gate_ref):
    x = x_ref[...]
    y = x * lax.rsqrt(jnp.mean(x * x, axis=-1, keepdims=True) + NORM_EPS) * g_ref[...]
    hb = y.astype(BF16)
    h_ref[...] = hb
    gate_ref[...] = jax.nn.sigmoid(_dot(hb, wg_ref[...]))


def _norm1(x2, g, wg):
    t = x2.shape[0]
    return pl.pallas_call(
        _norm1_kernel,
        grid=(t // TM,),
        in_specs=[
            pl.BlockSpec((TM, D_MODEL), lambda i: (i, 0)),
            pl.BlockSpec((1, D_MODEL), lambda i: (0, 0)),
            pl.BlockSpec((D_MODEL, NSA_KV_GROUPS * LANES), lambda i: (0, 0)),
        ],
        out_specs=[
            pl.BlockSpec((TM, D_MODEL), lambda i: (i, 0)),
            pl.BlockSpec((TM, NSA_KV_GROUPS * LANES), lambda i: (i, 0)),
        ],
        out_shape=[
            jax.ShapeDtypeStruct((t, D_MODEL), BF16),
            jax.ShapeDtypeStruct((t, NSA_KV_GROUPS * LANES), F32),
        ],
        compiler_params=_params(("parallel",)),
        name="norm1",
    )(x2, g, wg)


def _rope(x, cos, sin_signed):
    return x * cos + pltpu.roll(x, HEAD_DIM // 2, axis=1) * sin_signed


def _qproj_kernel(h_ref, w_ref, cos_ref, sin_ref, qraw_ref, qrot_ref):
    acc = _dot(h_ref[...], w_ref[...])
    qraw_ref[...] = acc.astype(BF16)
    cos = cos_ref[...]
    sin = sin_ref[...]
    for hh in range(TN // HEAD_DIM):
        sl = slice(hh * HEAD_DIM, (hh + 1) * HEAD_DIM)
        qrot_ref[:, sl] = _rope(acc[:, sl], cos, sin).astype(BF16)


def _qproj(h, wq, cos, sin, s):
    t = h.shape[0]
    ns = s // TM
    return pl.pallas_call(
        _qproj_kernel,
        grid=(Q_W // TN, t // TM),
        in_specs=[
            pl.BlockSpec((TM, D_MODEL), lambda j, i: (i, 0)),
            pl.BlockSpec((D_MODEL, TN), lambda j, i: (0, j)),
            pl.BlockSpec((TM, HEAD_DIM), lambda j, i: (i % ns, 0)),
            pl.BlockSpec((TM, HEAD_DIM), lambda j, i: (i % ns, 0)),
        ],
        out_specs=[
            pl.BlockSpec((TM, TN), lambda j, i: (i, j)),
            pl.BlockSpec((TM, TN), lambda j, i: (i, j)),
        ],
        out_shape=[jax.ShapeDtypeStruct((t, Q_W), BF16), jax.ShapeDtypeStruct((t, Q_W), BF16)],
        compiler_params=_params(("parallel", "parallel")),
        name="qproj",
    )(h, wq, cos, sin)


KV_KINDS = 6
KIND_K_SEL = 2
KIND_K_WIN = 4


def _kvproj_kernel(h_ref, w_ref, cos_ref, sin_ref, out_ref):
    kind = pl.program_id(0)
    acc = _dot(h_ref[...], w_ref[...])
    cos = cos_ref[...]
    sin = sin_ref[...]
    is_rope = jnp.logical_or(kind == KIND_K_SEL, kind == KIND_K_WIN)
    for g in range(NSA_KV_GROUPS):
        xh = acc[:, g * HEAD_DIM:(g + 1) * HEAD_DIM]
        out_ref[0, 0, g] = jnp.where(is_rope, _rope(xh, cos, sin), xh).astype(BF16)


def _kvproj(h, wkv, cos, sin, b, s):
    t = h.shape[0]
    ns = s // TM
    return pl.pallas_call(
        _kvproj_kernel,
        grid=(KV_KINDS, t // TM),
        in_specs=[
            pl.BlockSpec((TM, D_MODEL), lambda k, i: (i, 0)),
            pl.BlockSpec((D_MODEL, KV_W), lambda k, i: (0, k)),
            pl.BlockSpec((TM, HEAD_DIM), lambda k, i: (i % ns, 0)),
            pl.BlockSpec((TM, HEAD_DIM), lambda k, i: (i % ns, 0)),
        ],
        out_specs=pl.BlockSpec((1, 1, NSA_KV_GROUPS, TM, HEAD_DIM), lambda k, i: (k, i // ns, 0, i % ns, 0)),
        out_shape=jax.ShapeDtypeStruct((KV_KINDS, b, NSA_KV_GROUPS, s, HEAD_DIM), BF16),
        compiler_params=_params(("parallel", "parallel")),
        name="kvproj",
    )(h, wkv, cos, sin)


def _proj_kernel(h_ref, w_ref, o_ref):
    o_ref[...] = _dot(h_ref[...], w_ref[...]).astype(o_ref.dtype)


def _proj(h, w, name):
    t = h.shape[0]
    n = w.shape[1]
    return pl.pallas_call(
        _proj_kernel,
        grid=(n // TN, t // TM),
        in_specs=[
            pl.BlockSpec((TM, D_MODEL), lambda j, i: (i, 0)),
            pl.BlockSpec((D_MODEL, TN), lambda j, i: (0, j)),
        ],
        out_specs=pl.BlockSpec((TM, TN), lambda j, i: (i, j)),
        out_shape=jax.ShapeDtypeStruct((t, n), BF16),
        compiler_params=_params(("parallel", "parallel")),
        name=name,
    )(h, w)


def _cmpkv_kernel(c_ref, w1_ref, w2_ref, pe_ref, o_ref):
    c = c_ref[0, 0, 0]
    half = CMP_STRIDE * HEAD_DIM
    top = _dot(c, w1_ref[0, :half, :])
    bot = _dot(c, w1_ref[0, half:, :])
    n_rows = top.shape[0]
    bot_next = pltpu.roll(bot, n_rows - 1, axis=0)
    bias = _dot(pe_ref[0], w1_ref[0])[0:1, :]
    hid = jax.nn.gelu(top + bot_next + bias)
    o_ref[0, 0, 0] = _dot(hid.astype(BF16), w2_ref[0]).astype(BF16)


def _cmpkv(kv6, w1, w2, pe, b, s):
    nc = s // CMP_STRIDE
    kvc = kv6[:2].reshape(2, b, NSA_KV_GROUPS, nc, CMP_STRIDE * HEAD_DIM)
    return pl.pallas_call(
        _cmpkv_kernel,
        grid=(2, b, NSA_KV_GROUPS),
        in_specs=[
            pl.BlockSpec((1, 1, 1, nc, CMP_STRIDE * HEAD_DIM), lambda k, bi, g: (k, bi, g, 0, 0)),
            pl.BlockSpec((1, CMP_BLOCK * HEAD_DIM, CMP_HIDDEN), lambda k, bi, g: (k, 0, 0)),
            pl.BlockSpec((1, CMP_HIDDEN, HEAD_DIM), lambda k, bi, g: (k, 0, 0)),
            pl.BlockSpec((1, SUBLANES, CMP_BLOCK * HEAD_DIM), lambda k, bi, g: (k, 0, 0)),
        ],
        out_specs=pl.BlockSpec((1, 1, 1, nc, HEAD_DIM), lambda k, bi, g: (k, bi, g, 0, 0)),
        out_shape=jax.ShapeDtypeStruct((2, b, NSA_KV_GROUPS, nc, HEAD_DIM), BF16),
        compiler_params=_params(("parallel", "parallel", "parallel")),
        name="cmpkv",
    )(kvc, w1, w2, pe)


def _cmpattn_kernel(q_ref, kc_ref, vc_ref, mt_ref, ocmp_ref, msk_ref, *, n_sel):
    i = pl.program_id(2)
    tq = TQ_CMP
    q = q_ref[0]
    kc = kc_ref[0, 0, 0]
    vc = vc_ref[0, 0, 0]
    nc = kc.shape[0]
    t_row = i * tq + lax.broadcasted_iota(I32, (tq, 1), 0)
    n_lane = lax.broadcasted_iota(I32, (tq, nc), 1)
    cmask = (n_lane * CMP_STRIDE + (CMP_BLOCK - 1)) <= t_row
    psum = jnp.zeros((tq, nc), F32)
    for hh in range(NSA_GROUP):
        sl = slice(hh * HEAD_DIM, (hh + 1) * HEAD_DIM)
        s = _dot_nt(q[:, sl], kc) * SCALE
        s = jnp.where(cmask, s, NEG_INF)
        e = jnp.exp(s - jnp.max(s, axis=-1, keepdims=True))
        p = e / jnp.sum(e, axis=-1, keepdims=True)
        p = jnp.where(cmask, p, 0.0)
        ocmp_ref[0, :, sl] = _dot(p.astype(BF16), vc).astype(BF16)
        psum = psum + p
    imp_t = _dot_nt(mt_ref[...], psum.astype(BF16))
    imp_t = imp_t[:n_sel]
    jb = lax.broadcasted_iota(I32, (n_sel, tq), 0)
    cur = (i * tq + lax.broadcasted_iota(I32, (n_sel, tq), 1)) >> SEL_BLOCK_LOG2
    forced = (jb == 0) | (jb == cur) | (jb == cur - 1)
    valid = jb <= cur
    score = jnp.where(forced, FORCE_SCORE, jnp.where(valid, imp_t, NEG_INF))
    pad = jnp.zeros((LANES - n_sel, tq), F32)
    needs_rank = (i + 1) * tq > SEL_TOPK * SEL_BLOCK

    @pl.when(needs_rank)
    def _():
        rank = jnp.zeros((n_sel, tq), I32)
        for jp in range(n_sel):
            row = score[jp:jp + 1, :]
            beats = (row > score) | ((row == score) & (jb > jp))
            rank = rank + beats.astype(I32)
        member = (valid & (rank < SEL_TOPK)).astype(F32)
        msk_ref[0, 0] = jnp.concatenate([member, pad], axis=0).astype(BF16)

    @pl.when(jnp.logical_not(needs_rank))
    def _():
        msk_ref[0, 0] = jnp.concatenate([valid.astype(F32), pad], axis=0).astype(BF16)


def _cmpattn(q_raw, kvc, mt, b, s):
    nc = s // CMP_STRIDE
    n_sel = s // SEL_BLOCK
    q3 = q_raw.reshape(b, s, Q_W)
    return pl.pallas_call(
        functools.partial(_cmpattn_kernel, n_sel=n_sel),
        grid=(b, NSA_KV_GROUPS, s // TQ_CMP),
        in_specs=[
            pl.BlockSpec((1, TQ_CMP, GROUP_W), lambda bi, g, i: (bi, i, g)),
            pl.BlockSpec((1, 1, 1, nc, HEAD_DIM), lambda bi, g, i: (0, bi, g, 0, 0)),
            pl.BlockSpec((1, 1, 1, nc, HEAD_DIM), lambda bi, g, i: (1, bi, g, 0, 0)),
            pl.BlockSpec((LANES, nc), lambda bi, g, i: (0, 0)),
        ],
        out_specs=[
            pl.BlockSpec((1, TQ_CMP, GROUP_W), lambda bi, g, i: (bi, i, g)),
            pl.BlockSpec((1, 1, LANES, TQ_CMP), lambda bi, g, i: (bi, g, 0, i)),
        ],
        out_shape=[
            jax.ShapeDtypeStruct((b, s, Q_W), BF16),
            jax.ShapeDtypeStruct((b, NSA_KV_GROUPS, LANES, s), BF16),
        ],
        compiler_params=_params(("parallel", "parallel", "parallel")),
        name="cmpattn",
    )(q3, kvc, kvc, mt)


def _selwin_kernel(q_ref, ks_ref, vs_ref, kw_ref, vw_ref, msk_ref, emat_ref, ocmp_ref, gate_ref, o_ref,
                   acc_sc, accw_sc, sa_sc, sb_sc, ba_sc, bb_sc, *, seq):
    assert TQ == TK_SEL, "the window walk assumes key tile n covers the queries of grid step n"
    i = pl.program_id(2)
    tq = TQ
    q = q_ref[0]
    heads = [slice(hh * tq, (hh + 1) * tq) for hh in range(NSA_GROUP)]
    q_t = jnp.concatenate(
        [q[:, hh * HEAD_DIM:(hh + 1) * HEAD_DIM].astype(F32).T.astype(BF16) for hh in range(NSA_GROUP)], axis=1)
    t_lane = i * tq + lax.broadcasted_iota(I32, (1, tq), 1)

    c_exp = SCALE * LOG2_E
    last_tile = seq // TK_SEL - 1

    def make_branch(k_ref, v_ref, acc_ref, bias_of):
        def scores(j, s_ref, b_ref):
            off = pl.multiple_of(j * TK_SEL, TK_SEL)
            s_ref[...] = _dot(k_ref[0, 0, 0, pl.ds(off, TK_SEL), :], q_t)
            kpos = off + lax.broadcasted_iota(I32, (TK_SEL, tq), 0)
            b_ref[...] = jnp.where(bias_of(j, kpos), 0.0, NEG_INF)

        def absorb(j, s_ref, b_ref, carry):
            m_old, l_old = carry
            bias = b_ref[...]
            m_new, l_new, alphas, ps = [], [], [], []
            for hh in range(NSA_GROUP):
                sh = s_ref[:, heads[hh]] + bias
                mh = jnp.maximum(m_old[hh], jnp.max(sh, axis=0, keepdims=True))
                alpha = jnp.exp2((m_old[hh] - mh) * c_exp)
                p = jnp.exp2((sh - mh) * c_exp)
                m_new.append(mh)
                l_new.append(alpha * l_old[hh] + jnp.sum(p, axis=0, keepdims=True))
                alphas.append(alpha)
                ps.append(p.astype(BF16))
            p_t = jnp.concatenate(ps, axis=1)
            acc_ref[...] = acc_ref[...] * jnp.concatenate(alphas, axis=1) + _dot(v_ref[0, 0, j], p_t)
            return tuple(m_new), tuple(l_new)

        return scores, absorb

    m0 = tuple(jnp.full((1, tq), NEG_INF, F32) for _ in range(NSA_GROUP))
    l0 = tuple(jnp.zeros((1, tq), F32) for _ in range(NSA_GROUP))

    selm_t = msk_ref[0, 0]

    def sel_visible(j, kpos):
        picked = _dot(emat_ref[j], selm_t)
        return (picked > 0.5) & (kpos <= t_lane)

    sel_scores, sel_absorb = make_branch(ks_ref, vs_ref, acc_sc, sel_visible)
    acc_sc[...] = jnp.zeros(acc_sc.shape, F32)
    n_kv = (i * tq + tq - 1) // TK_SEL + 1

    sel_scores(0, sa_sc, ba_sc)

    def pair(jj, carry):
        j0 = 2 * jj
        j1 = jnp.minimum(j0 + 1, last_tile)
        sel_scores(j1, sb_sc, bb_sc)
        carry = sel_absorb(j0, sa_sc, ba_sc, carry)
        sel_scores(jnp.minimum(j0 + 2, last_tile), sa_sc, ba_sc)
        return sel_absorb(j1, sb_sc, bb_sc, carry)

    _, l_sel = lax.fori_loop(0, (n_kv + 1) >> 1, pair, (m0, l0))

    def win_visible(j, kpos):
        return (kpos <= t_lane) & (kpos > t_lane - WINDOW)

    win_scores, win_absorb = make_branch(kw_ref, vw_ref, accw_sc, win_visible)
    accw_sc[...] = jnp.zeros(accw_sc.shape, F32)
    n_back = WINDOW // TK_SEL
    tiles = [i] + [jnp.where(i >= kb, i - kb, kb) for kb in range(1, n_back + 1)]
    bufs = [(sa_sc, ba_sc), (sb_sc, bb_sc)]
    win_scores(tiles[0], *bufs[0])
    carry_w = (m0, l0)
    for n, tile in enumerate(tiles):
        if n + 1 < len(tiles):
            win_scores(tiles[n + 1], *bufs[(n + 1) % 2])
        carry_w = win_absorb(tile, *bufs[n % 2], carry_w)
    l_win = carry_w[1]

    gate = gate_ref[0]
    gate_t = gate.T
    ocmp = ocmp_ref[0]
    for hh in range(NSA_GROUP):
        cols = slice(hh * HEAD_DIM, (hh + 1) * HEAD_DIM)
        ls = l_sel[hh]
        o_sel_t = acc_sc[:, heads[hh]] * jnp.where(ls > 0.0, 1.0 / ls, 0.0)
        o_win_t = accw_sc[:, heads[hh]] * (1.0 / l_win[hh])
        z_t = gate_t[3 * hh + 1:3 * hh + 2, :] * o_sel_t + gate_t[3 * hh + 2:3 * hh + 3, :] * o_win_t
        o = gate[:, 3 * hh:3 * hh + 1] * ocmp[:, cols].astype(F32) + z_t.T
        o_ref[0, :, cols] = o.astype(BF16)


def _selwin(q_rot, kv6, msk, emat, ocmp, gates, b, s):
    q3 = q_rot.reshape(b, s, Q_W)
    g3 = gates.reshape(b, s, NSA_KV_GROUPS * LANES)

    def k_spec(kind):
        return pl.BlockSpec((1, 1, 1, s, HEAD_DIM), lambda bi, g, i: (kind, bi, g, 0, 0))

    def vt_tiles(kind, tile):
        return jnp.swapaxes(kv6[kind].reshape(b, NSA_KV_GROUPS, s // tile, tile, HEAD_DIM), -1, -2)

    def vt_spec(tile):
        return pl.BlockSpec((1, 1, s // tile, HEAD_DIM, tile), lambda bi, g, i: (bi, g, 0, 0, 0))

    return pl.pallas_call(
        functools.partial(_selwin_kernel, seq=s),
        grid=(b, NSA_KV_GROUPS, s // TQ),
        in_specs=[
            pl.BlockSpec((1, TQ, GROUP_W), lambda bi, g, i: (bi, i, g)),
            k_spec(2), vt_spec(TK_SEL), k_spec(4), vt_spec(TQ),
            pl.BlockSpec((1, 1, LANES, TQ), lambda bi, g, i: (bi, g, 0, i)),
            pl.BlockSpec((s // TK_SEL, TK_SEL, LANES), lambda bi, g, i: (0, 0, 0)),
            pl.BlockSpec((1, TQ, GROUP_W), lambda bi, g, i: (bi, i, g)),
            pl.BlockSpec((1, TQ, LANES), lambda bi, g, i: (bi, i, g)),
        ],
        out_specs=pl.BlockSpec((1, TQ, GROUP_W), lambda bi, g, i: (bi, i, g)),
        out_shape=jax.ShapeDtypeStruct((b, s, Q_W), BF16),
        scratch_shapes=[
            pltpu.VMEM((HEAD_DIM, NSA_GROUP * TQ), F32),
            pltpu.VMEM((HEAD_DIM, NSA_GROUP * TQ), F32),
            pltpu.VMEM((TK_SEL, NSA_GROUP * TQ), F32),
            pltpu.VMEM((TK_SEL, NSA_GROUP * TQ), F32),
            pltpu.VMEM((TK_SEL, TQ), F32),
            pltpu.VMEM((TK_SEL, TQ), F32),
        ],
        compiler_params=_params(("parallel", "parallel", "parallel")),
        name="selwin",
    )(q3, kv6, vt_tiles(3, TK_SEL), kv6, vt_tiles(5, TQ), msk, emat, ocmp, g3)


def _softplus(x):
    return jnp.maximum(x, 0.0) + jnp.log1p(jnp.exp(-jnp.abs(x)))


def _lru_kernel(lx_ref, ly_ref, cw_ref, cb_ref, wa_ref, ba_ref, wi_ref, bi_ref, lam_ref, o_ref, xe_sc, h_sc):
    ts = TS_LRU
    halo = SUBLANES

    @pl.when(pl.program_id(2) == 0)
    def _():
        xe_sc[0:halo, :] = jnp.zeros((halo, CW_LRU), F32)
        h_sc[...] = jnp.zeros(h_sc.shape, F32)

    x = lx_ref[0].astype(F32)
    xe_sc[halo:halo + ts, :] = x
    xc = cb_ref[...]
    for k in range(CONV_WIDTH):
        d = CONV_WIDTH - 1 - k
        xc = xc + xe_sc[pl.ds(halo - d, ts), :] * cw_ref[k:k + 1, :]
    xe_sc[0:halo, :] = x[ts - halo:ts]

    row_in_group = lax.broadcasted_iota(I32, (ts, LRU_BLOCK_W), 0) & (SUBLANES - 1)
    for n in range(CW_LRU // LRU_BLOCK_W):
        cols = slice(n * LRU_BLOCK_W, (n + 1) * LRU_BLOCK_W)
        xcn = xc[:, cols]
        xb = xcn.astype(BF16)
        r = jax.nn.sigmoid(_dot(xb, wa_ref[n]) + ba_ref[:, cols])
        ig = jax.nn.sigmoid(_dot(xb, wi_ref[n]) + bi_ref[:, cols])
        log_a = (-LRU_C * _softplus(-lam_ref[:, cols])) * r
        a = jnp.exp(log_a)
        u = jnp.sqrt(-jnp.tanh(log_a) * (a * a + 1.0)) * (ig * xcn)
        ca, cb = a, u
        for d in (1, 2, 4):
            ok = row_in_group >= d
            ca_sh = pltpu.roll(ca, d, axis=0)
            cb_sh = pltpu.roll(cb, d, axis=0)
            cb = jnp.where(ok, ca * cb_sh + cb, cb)
            ca = jnp.where(ok, ca * ca_sh, ca)
        hprev = h_sc[:, cols]
        hs = []
        for gi in range(ts // SUBLANES):
            rows = slice(gi * SUBLANES, (gi + 1) * SUBLANES)
            hg = ca[rows] * hprev + cb[rows]
            hs.append(hg)
            hprev = hg[SUBLANES - 1:SUBLANES]
        h_sc[:, cols] = hprev
        h = jnp.concatenate(hs, axis=0)
        o_ref[0, :, cols] = (h * jax.nn.gelu(ly_ref[0, :, cols].astype(F32))).astype(BF16)


def _lru(lxy, conv_w, conv_b, wa, ba, wi, bi, lam, b, s):
    lxy3 = lxy.reshape(b, s, 2 * LRU_WIDTH)
    ncw = LRU_WIDTH // CW_LRU
    nb = CW_LRU // LRU_BLOCK_W
    vec = pl.BlockSpec((1, CW_LRU), lambda bi_, c, t: (0, c))
    return pl.pallas_call(
        _lru_kernel,
        grid=(b, ncw, s // TS_LRU),
        in_specs=[
            pl.BlockSpec((1, TS_LRU, CW_LRU), lambda bi_, c, t: (bi_, t, c)),
            pl.BlockSpec((1, TS_LRU, CW_LRU), lambda bi_, c, t: (bi_, t, ncw + c)),
            pl.BlockSpec((CONV_WIDTH, CW_LRU), lambda bi_, c, t: (0, c)),
            vec,
            pl.BlockSpec((nb, LRU_BLOCK_W, LRU_BLOCK_W), lambda bi_, c, t: (c, 0, 0)),
            vec,
            pl.BlockSpec((nb, LRU_BLOCK_W, LRU_BLOCK_W), lambda bi_, c, t: (c, 0, 0)),
            vec,
            vec,
        ],
        out_specs=pl.BlockSpec((1, TS_LRU, CW_LRU), lambda bi_, c, t: (bi_, t, c)),
        out_shape=jax.ShapeDtypeStruct((b, s, LRU_WIDTH), BF16),
        scratch_shapes=[
            pltpu.VMEM((TS_LRU + SUBLANES, CW_LRU), F32),
            pltpu.VMEM((1, CW_LRU), F32),
        ],
        compiler_params=_params(("parallel", "parallel", "arbitrary")),
        name="lru",
    )(lxy3, lxy3, conv_w, conv_b, wa, ba, wi, bi, lam)


def _merge_kernel(h_ref, oa_ref, ob_ref, wga_ref, wgb_ref, wpa_ref, wpb_ref, o_ref):
    h = h_ref[...]
    ga = jax.nn.sigmoid(_dot(h, wga_ref[...]))
    gb = jax.nn.sigmoid(_dot(h, wgb_ref[...]))
    o_ref[...] = (ga * _dot(oa_ref[...], wpa_ref[...]) + gb * _dot(ob_ref[...], wpb_ref[...])).astype(BF16)


def _merge(h, oa, ob, wga, wgb, wpa, wpb):
    t = h.shape[0]
    row = pl.BlockSpec((TM, D_MODEL), lambda j, i: (i, 0))
    col = pl.BlockSpec((D_MODEL, TN), lambda j, i: (0, j))
    return pl.pallas_call(
        _merge_kernel,
        grid=(D_MODEL // TN, t // TM),
        in_specs=[row, row, row, col, col, col, col],
        out_specs=pl.BlockSpec((TM, TN), lambda j, i: (i, j)),
        out_shape=jax.ShapeDtypeStruct((t, D_MODEL), BF16),
        compiler_params=_params(("parallel", "parallel"), VMEM_LIMIT),
        name="merge",
    )(h, oa, ob, wga, wgb, wpa, wpb)


def _outrouter_kernel(mg_ref, x_ref, wo_ref, g2_ref, rw_ref, rb_ref, tri_ref,
                      x1_ref, h2_ref, tope_ref, gate_ref, rank_ref, cnt_ref, carry_sc):
    tm = TM_OUT

    @pl.when(pl.program_id(0) == 0)
    def _():
        carry_sc[...] = jnp.zeros(carry_sc.shape, F32)

    x1 = x_ref[...] + _dot(mg_ref[...], wo_ref[...])
    x1_ref[...] = x1
    y = x1 * lax.rsqrt(jnp.mean(x1 * x1, axis=-1, keepdims=True) + NORM_EPS) * g2_ref[...]
    h2_ref[...] = y
    logits = _dot(y.astype(BF16), rw_ref[...]) + rb_ref[...]
    lane = lax.broadcasted_iota(I32, (tm, LANES), 1)
    lane_f = lane.astype(F32)
    work = jnp.where(lane < N_EXPERTS, logits, -jnp.inf)
    vals, idxs = [], []
    for _k in range(TOP_K):
        m = jnp.max(work, axis=-1, keepdims=True)
        idx = jnp.min(jnp.where(work == m, lane_f, float(LANES)), axis=-1, keepdims=True).astype(I32)
        vals.append(m)
        idxs.append(idx)
        work = jnp.where(lane == idx, -jnp.inf, work)
    es = [jnp.exp(v - vals[0]) for v in vals]
    denom = es[0] + es[1] + es[2] + es[3]
    cnt = jnp.zeros((tm, LANES), F32)
    for idx in idxs:
        cnt = cnt + (lane == idx).astype(F32)
    before = _dot(tri_ref[...], cnt.astype(BF16)) + carry_sc[...]
    tope = jnp.zeros((tm, LANES), I32)
    gate = jnp.zeros((tm, LANES), F32)
    rank = jnp.zeros((tm, LANES), F32)
    for k in range(TOP_K):
        tope = jnp.where(lane == k, idxs[k], tope)
        gate = jnp.where(lane == k, es[k] / denom, gate)
        rk = jnp.sum(jnp.where(lane == idxs[k], before, 0.0), axis=-1, keepdims=True)
        rank = jnp.where(lane == k, rk, rank)
    tope_ref[...] = tope
    gate_ref[...] = gate
    rank_ref[...] = rank.astype(I32)
    carry = carry_sc[...] + jnp.sum(cnt, axis=0, keepdims=True)
    carry_sc[...] = carry
    cnt_ref[...] = jnp.broadcast_to(carry, cnt_ref.shape).astype(I32)


def _outrouter(merged, x2, wo, g2, rw, rb, tri):
    t = x2.shape[0]
    row_bf = pl.BlockSpec((TM_OUT, D_MODEL), lambda i: (i, 0))
    lane_blk = pl.BlockSpec((TM_OUT, LANES), lambda i: (i, 0))
    return pl.pallas_call(
        _outrouter_kernel,
        grid=(t // TM_OUT,),
        in_specs=[
            row_bf,
            row_bf,
            pl.BlockSpec((D_MODEL, D_MODEL), lambda i: (0, 0)),
            pl.BlockSpec((1, D_MODEL), lambda i: (0, 0)),
            pl.BlockSpec((D_MODEL, LANES), lambda i: (0, 0)),
            pl.BlockSpec((1, LANES), lambda i: (0, 0)),
            pl.BlockSpec((TM_OUT, TM_OUT), lambda i: (0, 0)),
        ],
        out_specs=[row_bf, row_bf, lane_blk, lane_blk, lane_blk,
                   pl.BlockSpec((SUBLANES, LANES), lambda i: (0, 0))],
        out_shape=[
            jax.ShapeDtypeStruct((t, D_MODEL), F32),
            jax.ShapeDtypeStruct((t, D_MODEL), F32),
            jax.ShapeDtypeStruct((t, LANES), I32),
            jax.ShapeDtypeStruct((t, LANES), F32),
            jax.ShapeDtypeStruct((t, LANES), I32),
            jax.ShapeDtypeStruct((SUBLANES, LANES), I32),
        ],
        scratch_shapes=[pltpu.VMEM((1, LANES), F32)],
        compiler_params=_params(("arbitrary",), VMEM_LIMIT),
        name="outrouter",
    )(merged, x2, wo, g2, rw, rb, tri)


def _row_copy(src_hbm, dst, src_row, dst_row, sem):
    return pltpu.make_async_copy(src_hbm.at[pl.ds(src_row, 1)], dst.at[pl.ds(dst_row, 1)], sem)


def _issue_rows(n_rows, copy_of_row):
    def body(r, carry):
        copy_of_row(r).start()
        return carry

    lax.fori_loop(0, n_rows, body, 0, unroll=DMA_UNROLL)


def _drain_rows(n_rows, any_row_copy):
    def body(r, carry):
        any_row_copy.wait()
        return carry

    lax.fori_loop(0, n_rows, body, 0, unroll=DMA_UNROLL)


def _experts_kernel(ce_ref, na_ref, tokc_ref, tokn_ref, h_hbm, wg_hbm, bg_ref, wu_hbm, bu_ref, wd_hbm, bd_ref,
                    y_ref, xbuf, sem, wg_sc, wu_sc, wd_sc, sg_sc, su_sc, sd_sc, wsem):
    c = pl.program_id(0)
    n_act = na_ref[0]
    slot = c & 1
    last_c = pl.num_programs(0) - 1
    e = ce_ref[c]
    e_next = ce_ref[jnp.minimum(c + 1, last_c)]
    is_first = jnp.logical_or(c == 0, ce_ref[jnp.maximum(c - 1, 0)] != e)
    next_switch = jnp.logical_and(c + 1 < n_act, e_next != e)

    def gather(tok_ref, s_):
        _issue_rows(MOE_ROWS, lambda r: _row_copy(h_hbm, xbuf.at[s_], tok_ref[0, 0, r], r, sem.at[s_]))

    def stage_copies(ex, s, sl):
        span = pl.ds(s * W_STAGE, W_STAGE)
        full = pl.ds(0, D_MODEL)
        return (pltpu.make_async_copy(wg_hbm.at[ex, full, span], sg_sc.at[sl], wsem.at[sl, 0]),
                pltpu.make_async_copy(wu_hbm.at[ex, full, span], su_sc.at[sl], wsem.at[sl, 1]),
                pltpu.make_async_copy(wd_hbm.at[ex, span, full], sd_sc.at[sl], wsem.at[sl, 2]))

    def start_stage(ex, s, sl):
        for cp in stage_copies(ex, s, sl):
            cp.start()

    def prefetch_next():
        for s in range(N_WSLOTS):
            start_stage(e_next, s, s)

    def ffn_chunk(x, acc, lo, width):
        cols = slice(lo, lo + width)
        g = _dot(x, wg_sc[:, cols]) + bg_ref[0, :, cols]
        u = _dot(x, wu_sc[:, cols]) + bu_ref[0, :, cols]
        g = jnp.minimum(g, SWIGLU_LIMIT)
        u = jnp.clip(u, -SWIGLU_LIMIT, SWIGLU_LIMIT)
        act = (u + 1.0) * (g * jax.nn.sigmoid(SWIGLU_ALPHA * g))
        return acc + _dot(act.astype(BF16), wd_sc[cols, :])

    @pl.when(jnp.logical_and(c == 0, n_act > 0))
    def _():
        gather(tokc_ref, 0)
        for s in range(N_WSLOTS):
            start_stage(e, s, s)

    @pl.when(c + 1 < n_act)
    def _():
        gather(tokn_ref, 1 - slot)

    @pl.when(c < n_act)
    def _():
        _drain_rows(MOE_ROWS, _row_copy(h_hbm, xbuf.at[slot], 0, 0, sem.at[slot]))
        x = xbuf[slot].astype(BF16)

        @pl.when(is_first)
        def _():
            acc = jnp.zeros((MOE_ROWS, D_MODEL), F32)
            for s in range(D_FF // W_STAGE):
                sl = s % N_WSLOTS
                span = slice(s * W_STAGE, (s + 1) * W_STAGE)
                for cp in stage_copies(e, s, sl):
                    cp.wait()
                wg_sc[:, span] = sg_sc[sl].astype(BF16)
                wu_sc[:, span] = su_sc[sl].astype(BF16)
                wd_sc[span, :] = sd_sc[sl].astype(BF16)
                if s + N_WSLOTS < D_FF // W_STAGE:
                    start_stage(e, s + N_WSLOTS, sl)
                acc = ffn_chunk(x, acc, s * W_STAGE, W_STAGE)
            y_ref[...] = acc + bd_ref[0]

            @pl.when(next_switch)
            def _():
                prefetch_next()

        @pl.when(jnp.logical_not(is_first))
        def _():
            @pl.when(next_switch)
            def _():
                prefetch_next()

            acc = jnp.zeros((MOE_ROWS, D_MODEL), F32)
            for f in range(D_FF // FF_CHUNK):
                acc = ffn_chunk(x, acc, f * FF_CHUNK, FF_CHUNK)
            y_ref[...] = acc + bd_ref[0]

    @pl.when(c >= n_act)
    def _():
        y_ref[...] = jnp.zeros(y_ref.shape, y_ref.dtype)


def _experts(chunk_e, n_active, row_tok, h2, wg, bg, wu, bu, wd, bd, n_chunks):
    def wspec():
        return pl.BlockSpec(memory_space=pl.ANY)

    def bspec():
        return pl.BlockSpec((1, 1, D_FF), lambda c, ce, na: (ce[c], 0, 0))

    tok3 = row_tok.reshape(n_chunks, 1, MOE_ROWS)
    grid_spec = pltpu.PrefetchScalarGridSpec(
        num_scalar_prefetch=2,
        grid=(n_chunks,),
        in_specs=[
            pl.BlockSpec((1, 1, MOE_ROWS), lambda c, ce, na: (c, 0, 0), memory_space=pltpu.SMEM),
            pl.BlockSpec((1, 1, MOE_ROWS), lambda c, ce, na: (jnp.minimum(c + 1, n_chunks - 1), 0, 0),
                         memory_space=pltpu.SMEM),
            pl.BlockSpec(memory_space=pl.ANY),
            wspec(), bspec(),
            wspec(), bspec(),
            wspec(), bspec(),
        ],
        out_specs=pl.BlockSpec((MOE_ROWS, D_MODEL), lambda c, ce, na: (c, 0)),
        scratch_shapes=[
            pltpu.VMEM((2, MOE_ROWS, D_MODEL), F32),
            pltpu.SemaphoreType.DMA((2,)),
            pltpu.VMEM((D_MODEL, D_FF), BF16),
            pltpu.VMEM((D_MODEL, D_FF), BF16),
            pltpu.VMEM((D_FF, D_MODEL), BF16),
            pltpu.VMEM((N_WSLOTS, D_MODEL, W_STAGE), F32),
            pltpu.VMEM((N_WSLOTS, D_MODEL, W_STAGE), F32),
            pltpu.VMEM((N_WSLOTS, W_STAGE, D_MODEL), F32),
            pltpu.SemaphoreType.DMA((N_WSLOTS, 3)),
        ],
    )
    return pl.pallas_call(
        _experts_kernel,
        grid_spec=grid_spec,
        out_shape=jax.ShapeDtypeStruct((n_chunks * MOE_ROWS, D_MODEL), F32),
        compiler_params=_params(("arbitrary",), VMEM_LIMIT_EXPERTS),
        name="experts",
    )(chunk_e, n_active, tok3, tok3, h2, wg, bg, wu, bu, wd, bd)


def _combine_kernel(destc_ref, destn_ref, gate_ref, x1_ref, gf_ref, y_hbm, o_ref, ybuf, sem):
    i = pl.program_id(0)
    slot = i & 1
    n = TB_COMBINE * TOP_K

    def gather(dest_ref, s_):
        _issue_rows(n, lambda r: _row_copy(y_hbm, ybuf.at[s_, r & (TOP_K - 1)], dest_ref[0, 0, r],
                                           r >> TOP_K_LOG2, sem.at[s_]))

    @pl.when(i == 0)
    def _():
        gather(destc_ref, 0)

    @pl.when(i + 1 < pl.num_programs(0))
    def _():
        gather(destn_ref, 1 - slot)

    _drain_rows(n, _row_copy(y_hbm, ybuf.at[slot, 0], 0, 0, sem.at[slot]))
    gate = gate_ref[...]
    out = x1_ref[...]
    for k in range(TOP_K):
        out = out + gate[:, k:k + 1] * ybuf[slot, k]
    o_ref[...] = out * lax.rsqrt(jnp.mean(out * out, axis=-1, keepdims=True) + NORM_EPS) * gf_ref[...]


def _combine(dest, gate, x1, gf, y_rows):
    t = x1.shape[0]
    nt = t // TB_COMBINE
    dest3 = dest.reshape(nt, 1, TB_COMBINE * TOP_K)
    return pl.pallas_call(
        _combine_kernel,
        grid=(nt,),
        in_specs=[
            pl.BlockSpec((1, 1, TB_COMBINE * TOP_K), lambda i: (i, 0, 0), memory_space=pltpu.SMEM),
            pl.BlockSpec((1, 1, TB_COMBINE * TOP_K), lambda i: (jnp.minimum(i + 1, nt - 1), 0, 0),
                         memory_space=pltpu.SMEM),
            pl.BlockSpec((TB_COMBINE, LANES), lambda i: (i, 0)),
            pl.BlockSpec((TB_COMBINE, D_MODEL), lambda i: (i, 0)),
            pl.BlockSpec((1, D_MODEL), lambda i: (0, 0)),
            pl.BlockSpec(memory_space=pl.ANY),
        ],
        out_specs=pl.BlockSpec((TB_COMBINE, D_MODEL), lambda i: (i, 0)),
        out_shape=jax.ShapeDtypeStruct((t, D_MODEL), F32),
        scratch_shapes=[
            pltpu.VMEM((2, TOP_K, TB_COMBINE, D_MODEL), F32),
            pltpu.SemaphoreType.DMA((2,)),
        ],
        compiler_params=_params(("arbitrary",)),
        name="combine",
    )(dest3, dest3, gate, x1, gf, y_rows)


def _rope_tables(s):
    pos = jnp.arange(s, dtype=F32)
    inv = ROPE_THETA ** (-jnp.arange(0, HEAD_DIM, 2, dtype=F32) / HEAD_DIM)
    ang = pos[:, None] * inv[None, :]
    ang = jnp.concatenate([ang, ang], axis=-1)
    sign = jnp.where(jnp.arange(HEAD_DIM) < HEAD_DIM // 2, -1.0, 1.0).astype(F32)
    return jnp.cos(ang), jnp.sin(ang) * sign[None, :]


def _cmp_to_sel_t(nc, n_sel):
    cs = np.arange(nc) * CMP_STRIDE
    ss = np.arange(n_sel) * SEL_BLOCK
    ov = np.minimum(cs[:, None] + CMP_BLOCK, ss[None, :] + SEL_BLOCK) - np.maximum(cs[:, None], ss[None, :])
    m = np.clip(ov, 0, None) / CMP_BLOCK
    m[nc - 1, :] = 0.0
    out = np.zeros((LANES, nc), np.float32)
    out[:n_sel] = m.T
    return jnp.asarray(out, dtype=BF16)


def _block_expand(s):
    key = np.arange(s).reshape(s // TK_SEL, TK_SEL, 1)
    blk = np.arange(LANES).reshape(1, 1, LANES)
    return jnp.asarray((key // SEL_BLOCK == blk).astype(np.float32), dtype=BF16)


def kernel(x, norm1_g, w_in, cmp_pe_k, cmp_pe_v, cmp_k_w1, cmp_k_w2, cmp_v_w1, cmp_v_w2, lru_conv_w, lru_conv_b, lru_wa, lru_ba, lru_wi, lru_bi, lru_lambda, w_br_attn, w_br_lru, w_out, norm2_g, router_w, router_b, exp_w_gate, exp_b_gate, exp_w_up, exp_b_up, exp_w_down, exp_b_down, normf_g):
    b, s, d = x.shape
    assert d == D_MODEL and norm1_g.shape[0] == 1, "single-layer block of width D_MODEL"
    assert s % TM == 0 and s % TS_LRU == 0 and s >= WIN_KEYS and s // SEL_BLOCK <= LANES
    assert s % (2 * TK_SEL) == 0, "the selected branch walks key tiles in pairs"
    t = b * s
    x2 = x.reshape(t, d)

    w_in0 = w_in[0]
    o = 0
    wq = w_in0[:, o:o + Q_W].astype(BF16); o += Q_W
    wkv = w_in0[:, o:o + KV_KINDS * KV_W].astype(BF16); o += KV_KINDS * KV_W
    n_g = 3 * NSA_Q_HEADS
    wg_nsa = w_in0[:, o:o + n_g].reshape(d, NSA_KV_GROUPS, 3 * NSA_GROUP); o += n_g
    wg_nsa = jnp.pad(wg_nsa, ((0, 0), (0, 0), (0, LANES - 3 * NSA_GROUP))).reshape(d, NSA_KV_GROUPS * LANES).astype(BF16)
    wlxy = w_in0[:, o:o + 2 * LRU_WIDTH].astype(BF16); o += 2 * LRU_WIDTH
    wga = w_in0[:, o:o + D_MODEL].astype(BF16); o += D_MODEL
    wgb = w_in0[:, o:o + D_MODEL].astype(BF16); o += D_MODEL

    cos, sin_signed = _rope_tables(s)
    nc = s // CMP_STRIDE
    n_sel = s // SEL_BLOCK

    h, gates = _norm1(x2, norm1_g, wg_nsa)
    q_raw, q_rot = _qproj(h, wq, cos, sin_signed, s)
    kv6 = _kvproj(h, wkv, cos, sin_signed, b, s)
    lxy = _proj(h, wlxy, "lruproj")

    cw1 = jnp.stack([cmp_k_w1[0], cmp_v_w1[0]]).astype(BF16)
    cw2 = jnp.stack([cmp_k_w2[0], cmp_v_w2[0]]).astype(BF16)
    pe = jnp.stack([cmp_pe_k[0], cmp_pe_v[0]]).reshape(2, 1, CMP_BLOCK * HEAD_DIM)
    pe = jnp.pad(pe, ((0, 0), (0, SUBLANES - 1), (0, 0))).astype(BF16)
    kvc = _cmpkv(kv6, cw1, cw2, pe, b, s)
    ocmp, msk = _cmpattn(q_raw, kvc, _cmp_to_sel_t(nc, n_sel), b, s)
    o_a = _selwin(q_rot, kv6, msk, _block_expand(s), ocmp, gates, b, s)

    o_b = _lru(lxy, lru_conv_w[0], lru_conv_b, lru_wa[0].astype(BF16), lru_ba[0].reshape(1, LRU_WIDTH),
               lru_wi[0].astype(BF16), lru_bi[0].reshape(1, LRU_WIDTH), lru_lambda, b, s)

    merged = _merge(h, o_a.reshape(t, Q_W), o_b.reshape(t, LRU_WIDTH), wga, wgb,
                    w_br_attn[0].astype(BF16), w_br_lru[0].astype(BF16))

    rw = jnp.pad(router_w[0], ((0, 0), (0, LANES - N_EXPERTS))).astype(BF16)
    rb = jnp.pad(router_b, ((0, 0), (0, LANES - N_EXPERTS)))
    tri = jnp.asarray(np.tril(np.ones((TM_OUT, TM_OUT), np.float32), -1), dtype=BF16)
    x1, h2, tope, gate, rank, cnt = _outrouter(merged, x2, w_out[0].astype(BF16), norm2_g, rw, rb, tri)

    counts = cnt[0, :N_EXPERTS]
    padded = (counts + MOE_ROWS - 1) // MOE_ROWS * MOE_ROWS
    pend = jnp.cumsum(padded)
    pstart = pend - padded
    n_rows = t * TOP_K + N_EXPERTS * MOE_ROWS
    n_chunks = n_rows // MOE_ROWS
    dest = (pstart[tope[:, :TOP_K]] + rank[:, :TOP_K]).astype(I32)
    tok_of = jnp.broadcast_to(jnp.arange(t, dtype=I32)[:, None], (t, TOP_K))
    row_tok = jnp.zeros((n_rows,), I32).at[dest.reshape(-1)].set(tok_of.reshape(-1))
    chunk_e = jnp.minimum(jnp.sum((jnp.arange(n_chunks) * MOE_ROWS)[:, None] >= pend[None, :], axis=1),
                          N_EXPERTS - 1).astype(I32)
    n_active = (pend[-1:] // MOE_ROWS).astype(I32)

    y_rows = _experts(chunk_e, n_active, row_tok, h2,
                      exp_w_gate.reshape(N_EXPERTS, D_MODEL, D_FF), exp_b_gate[0].reshape(N_EXPERTS, 1, D_FF),
                      exp_w_up.reshape(N_EXPERTS, D_MODEL, D_FF), exp_b_up[0].reshape(N_EXPERTS, 1, D_FF),
                      exp_w_down.reshape(N_EXPERTS, D_FF, D_MODEL), exp_b_down[0].reshape(N_EXPERTS, 1, D_MODEL),
                      n_chunks)
    out = _combine(dest, gate, x1, normf_g.reshape(1, D_MODEL), y_rows)
    return out.reshape(b, s, d)
```

```python
import functools

import numpy as np
import jax
import jax.numpy as jnp
from jax import lax
from jax.experimental import pallas as pl
from jax.experimental.pallas import tpu as pltpu

F32 = jnp.float32
BF16 = jnp.bfloat16
I32 = jnp.int32

D_MODEL = 2048
HEAD_DIM = 128
NSA_Q_HEADS = 16
NSA_KV_GROUPS = 4
NSA_GROUP = NSA_Q_HEADS // NSA_KV_GROUPS
CMP_BLOCK = 32
CMP_STRIDE = 16
CMP_HIDDEN = 256
SEL_BLOCK = 64
SEL_BLOCK_LOG2 = 6
SEL_TOPK = 16
WINDOW = 512
ROPE_THETA = 10000.0
LRU_WIDTH = 2048
LRU_BLOCKS = 16
LRU_BLOCK_W = LRU_WIDTH // LRU_BLOCKS
CONV_WIDTH = 4
LRU_C = 8.0
N_EXPERTS = 32
TOP_K = 4
TOP_K_LOG2 = 2
D_FF = 2048
SWIGLU_ALPHA = 1.702
SWIGLU_LIMIT = 7.0
NORM_EPS = 1e-6
NEG_INF = -1e30
FORCE_SCORE = 1e6

Q_W = NSA_Q_HEADS * HEAD_DIM
KV_W = NSA_KV_GROUPS * HEAD_DIM
GROUP_W = NSA_GROUP * HEAD_DIM
SCALE = HEAD_DIM ** -0.5
LOG2_E = 1.4426950408889634
LANES = 128
SUBLANES = 8

TM = 1024
TN = 512
TQ_CMP = 256
TQ = 256
TQ_LOG2 = 8
TK_SEL = 256
WIN_KEYS = WINDOW + TQ
VT_ROWS = HEAD_DIM + 16
TS_LRU = 256
CW_LRU = 512
TM_OUT = 256
MOE_ROWS = 256
FF_CHUNK = 512
TB_COMBINE = 128
DMA_UNROLL = 8
W_STAGE = 256
N_WSLOTS = 2
VMEM_LIMIT = 56 * 1024 * 1024
VMEM_LIMIT_EXPERTS = 60 * 1024 * 1024


def _params(sem, vmem=None):
    return pltpu.CompilerParams(dimension_semantics=sem, vmem_limit_bytes=vmem)


def _dot(a, b):
    return jnp.dot(a, b, preferred_element_type=F32)


def _dot_nt(a, b, precision=None):
    return lax.dot_general(a, b, (((1,), (1,)), ((), ())), precision=precision, preferred_element_type=F32)


def _norm1_kernel(x_ref, g_ref, wg_ref, h_ref, gate_ref):
    x = x_ref[...]
    y = x * lax.rsqrt(jnp.mean(x * x, axis=-1, keepdims=True) + NORM_EPS) * g_ref[...]
    hb = y.astype(BF16)
    h_ref[...] = hb
    gate_ref[...] = jax.nn.sigmoid(_dot(hb, wg_ref[...]))


def _norm1(x2, g, wg):
    t = x2.shape[0]
    return pl.pallas_call(
        _norm1_kernel,
        grid=(t // TM,),
        in_specs=[
            pl.BlockSpec((TM, D_MODEL), lambda i: (i, 0)),
            pl.BlockSpec((1, D_MODEL), lambda i: (0, 0)),
            pl.BlockSpec((D_MODEL, NSA_KV_GROUPS * LANES), lambda i: (0, 0)),
        ],
        out_specs=[
            pl.BlockSpec((TM, D_MODEL), lambda i: (i, 0)),
            pl.BlockSpec((TM, NSA_KV_GROUPS * LANES), lambda i: (i, 0)),
        ],
        out_shape=[
            jax.ShapeDtypeStruct((t, D_MODEL), BF16),
            jax.ShapeDtypeStruct((t, NSA_KV_GROUPS * LANES), F32),
        ],
        compiler_params=_params(("parallel",)),
        name="norm1",
    )(x2, g, wg)


def _rope(x, cos, sin_signed):
    return x * cos + pltpu.roll(x, HEAD_DIM // 2, axis=1) * sin_signed


def _qproj_kernel(h_ref, w_ref, cos_ref, sin_ref, qraw_ref, qrot_ref):
    acc = _dot(h_ref[...], w_ref[...])
    qraw_ref[...] = acc.astype(BF16)
    cos = cos_ref[...]
    sin = sin_ref[...]
    for hh in range(TN // HEAD_DIM):
        sl = slice(hh * HEAD_DIM, (hh + 1) * HEAD_DIM)
        qrot_ref[:, sl] = _rope(acc[:, sl], cos, sin).astype(BF16)


def _qproj(h, wq, cos, sin, s):
    t = h.shape[0]
    ns = s // TM
    return pl.pallas_call(
        _qproj_kernel,
        grid=(Q_W // TN, t // TM),
        in_specs=[
            pl.BlockSpec((TM, D_MODEL), lambda j, i: (i, 0)),
            pl.BlockSpec((D_MODEL, TN), lambda j, i: (0, j)),
            pl.BlockSpec((TM, HEAD_DIM), lambda j, i: (i % ns, 0)),
            pl.BlockSpec((TM, HEAD_DIM), lambda j, i: (i % ns, 0)),
        ],
        out_specs=[
            pl.BlockSpec((TM, TN), lambda j, i: (i, j)),
            pl.BlockSpec((TM, TN), lambda j, i: (i, j)),
        ],
        out_shape=[jax.ShapeDtypeStruct((t, Q_W), BF16), jax.ShapeDtypeStruct((t, Q_W), BF16)],
        compiler_params=_params(("parallel", "parallel")),
        name="qproj",
    )(h, wq, cos, sin)


KV_KINDS = 6
KIND_K_SEL = 2
KIND_K_WIN = 4


def _kvproj_kernel(h_ref, w_ref, cos_ref, sin_ref, out_ref):
    kind = pl.program_id(0)
    acc = _dot(h_ref[...], w_ref[...])
    cos = cos_ref[...]
    sin = sin_ref[...]
    is_rope = jnp.logical_or(kind == KIND_K_SEL, kind == KIND_K_WIN)
    for g in range(NSA_KV_GROUPS):
        xh = acc[:, g * HEAD_DIM:(g + 1) * HEAD_DIM]
        out_ref[0, 0, g] = jnp.where(is_rope, _rope(xh, cos, sin), xh).astype(BF16)


def _kvproj(h, wkv, cos, sin, b, s):
    t = h.shape[0]
    ns = s // TM
    return pl.pallas_call(
        _kvproj_kernel,
        grid=(KV_KINDS, t // TM),
        in_specs=[
            pl.BlockSpec((TM, D_MODEL), lambda k, i: (i, 0)),
            pl.BlockSpec((D_MODEL, KV_W), lambda k, i: (0, k)),
            pl.BlockSpec((TM, HEAD_DIM), lambda k, i: (i % ns, 0)),
            pl.BlockSpec((TM, HEAD_DIM), lambda k, i: (i % ns, 0)),
        ],
        out_specs=pl.BlockSpec((1, 1, NSA_KV_GROUPS, TM, HEAD_DIM), lambda k, i: (k, i // ns, 0, i % ns, 0)),
        out_shape=jax.ShapeDtypeStruct((KV_KINDS, b, NSA_KV_GROUPS, s, HEAD_DIM), BF16),
        compiler_params=_params(("parallel", "parallel")),
        name="kvproj",
    )(h, wkv, cos, sin)


def _proj_kernel(h_ref, w_ref, o_ref):
    o_ref[...] = _dot(h_ref[...], w_ref[...]).astype(o_ref.dtype)


def _proj(h, w, name):
    t = h.shape[0]
    n = w.shape[1]
    return pl.pallas_call(
        _proj_kernel,
        grid=(n // TN, t // TM),
        in_specs=[
            pl.BlockSpec((TM, D_MODEL), lambda j, i: (i, 0)),
            pl.BlockSpec((D_MODEL, TN), lambda j, i: (0, j)),
        ],
        out_specs=pl.BlockSpec((TM, TN), lambda j, i: (i, j)),
        out_shape=jax.ShapeDtypeStruct((t, n), BF16),
        compiler_params=_params(("parallel", "parallel")),
        name=name,
    )(h, w)


def _cmpkv_kernel(c_ref, w1_ref, w2_ref, pe_ref, o_ref):
    c = c_ref[0, 0, 0]
    half = CMP_STRIDE * HEAD_DIM
    top = _dot(c, w1_ref[0, :half, :])
    bot = _dot(c, w1_ref[0, half:, :])
    n_rows = top.shape[0]
    bot_next = pltpu.roll(bot, n_rows - 1, axis=0)
    bias = _dot(pe_ref[0], w1_ref[0])[0:1, :]
    hid = jax.nn.gelu(top + bot_next + bias)
    o_ref[0, 0, 0] = _dot(hid.astype(BF16), w2_ref[0]).astype(BF16)


def _cmpkv(kv6, w1, w2, pe, b, s):
    nc = s // CMP_STRIDE
    kvc = kv6[:2].reshape(2, b, NSA_KV_GROUPS, nc, CMP_STRIDE * HEAD_DIM)
    return pl.pallas_call(
        _cmpkv_kernel,
        grid=(2, b, NSA_KV_GROUPS),
        in_specs=[
            pl.BlockSpec((1, 1, 1, nc, CMP_STRIDE * HEAD_DIM), lambda k, bi, g: (k, bi, g, 0, 0)),
            pl.BlockSpec((1, CMP_BLOCK * HEAD_DIM, CMP_HIDDEN), lambda k, bi, g: (k, 0, 0)),
            pl.BlockSpec((1, CMP_HIDDEN, HEAD_DIM), lambda k, bi, g: (k, 0, 0)),
            pl.BlockSpec((1, SUBLANES, CMP_BLOCK * HEAD_DIM), lambda k, bi, g: (k, 0, 0)),
        ],
        out_specs=pl.BlockSpec((1, 1, 1, nc, HEAD_DIM), lambda k, bi, g: (k, bi, g, 0, 0)),
        out_shape=jax.ShapeDtypeStruct((2, b, NSA_KV_GROUPS, nc, HEAD_DIM), BF16),
        compiler_params=_params(("parallel", "parallel", "parallel")),
        name="cmpkv",
    )(kvc, w1, w2, pe)


def _cmpattn_kernel(q_ref, kc_ref, vc_ref, mt_ref, ocmp_ref, msk_ref, *, n_sel):
    i = pl.program_id(2)
    tq = TQ_CMP
    q = q_ref[0]
    kc = kc_ref[0, 0, 0]
    vc = vc_ref[0, 0, 0]
    nc = kc.shape[0]
    t_row = i * tq + lax.broadcasted_iota(I32, (tq, 1), 0)
    n_lane = lax.broadcasted_iota(I32, (tq, nc), 1)
    cmask = (n_lane * CMP_STRIDE + (CMP_BLOCK - 1)) <= t_row
    psum = jnp.zeros((tq, nc), F32)
    for hh in range(NSA_GROUP):
        sl = slice(hh * HEAD_DIM, (hh + 1) * HEAD_DIM)
        s = _dot_nt(q[:, sl], kc) * SCALE
        s = jnp.where(cmask, s, NEG_INF)
        e = jnp.exp(s - jnp.max(s, axis=-1, keepdims=True))
        p = e / jnp.sum(e, axis=-1, keepdims=True)
        p = jnp.where(cmask, p, 0.0)
        ocmp_ref[0, :, sl] = _dot(p.astype(BF16), vc).astype(BF16)
        psum = psum + p
    imp_t = _dot_nt(mt_ref[...], psum.astype(BF16))
    imp_t = imp_t[:n_sel]
    jb = lax.broadcasted_iota(I32, (n_sel, tq), 0)
    cur = (i * tq + lax.broadcasted_iota(I32, (n_sel, tq), 1)) >> SEL_BLOCK_LOG2
    forced = (jb == 0) | (jb == cur) | (jb == cur - 1)
    valid = jb <= cur
    score = jnp.where(forced, FORCE_SCORE, jnp.where(valid, imp_t, NEG_INF))
    pad = jnp.zeros((LANES - n_sel, tq), F32)
    needs_rank = (i + 1) * tq > SEL_TOPK * SEL_BLOCK

    @pl.when(needs_rank)
    def _():
        n_grp = n_sel // SUBLANES
        grp = [score[gi * SUBLANES:(gi + 1) * SUBLANES] for gi in range(n_grp)]
        jb_grp = lax.broadcasted_iota(I32, (SUBLANES, tq), 0)
        ranks = [jnp.zeros((SUBLANES, tq), F32) for _ in range(n_grp)]
        for jp in range(n_sel):
            row = score[jp:jp + 1, :]
            for gi in range(n_grp):
                lo = gi * SUBLANES
                if lo > jp:
                    beats = row >= grp[gi]
                elif lo + SUBLANES - 1 <= jp:
                    beats = row > grp[gi]
                else:
                    beats = (row > grp[gi]) | ((row == grp[gi]) & (jb_grp > jp - lo))
                ranks[gi] = ranks[gi] + jnp.where(beats, 1.0, 0.0)
        rank = jnp.concatenate(ranks, axis=0)
        member = (valid & (rank < float(SEL_TOPK))).astype(F32)
        msk_ref[0, 0] = jnp.concatenate([member, pad], axis=0).astype(BF16)

    @pl.when(jnp.logical_not(needs_rank))
    def _():
        msk_ref[0, 0] = jnp.concatenate([valid.astype(F32), pad], axis=0).astype(BF16)


def _cmpattn(q_raw, kvc, mt, b, s):
    nc = s // CMP_STRIDE
    n_sel = s // SEL_BLOCK
    q3 = q_raw.reshape(b, s, Q_W)
    return pl.pallas_call(
        functools.partial(_cmpattn_kernel, n_sel=n_sel),
        grid=(b, NSA_KV_GROUPS, s // TQ_CMP),
        in_specs=[
            pl.BlockSpec((1, TQ_CMP, GROUP_W), lambda bi, g, i: (bi, i, g)),
            pl.BlockSpec((1, 1, 1, nc, HEAD_DIM), lambda bi, g, i: (0, bi, g, 0, 0)),
            pl.BlockSpec((1, 1, 1, nc, HEAD_DIM), lambda bi, g, i: (1, bi, g, 0, 0)),
            pl.BlockSpec((LANES, nc), lambda bi, g, i: (0, 0)),
        ],
        out_specs=[
            pl.BlockSpec((1, TQ_CMP, GROUP_W), lambda bi, g, i: (bi, i, g)),
            pl.BlockSpec((1, 1, LANES, TQ_CMP), lambda bi, g, i: (bi, g, 0, i)),
        ],
        out_shape=[
            jax.ShapeDtypeStruct((b, s, Q_W), BF16),
            jax.ShapeDtypeStruct((b, NSA_KV_GROUPS, LANES, s), BF16),
        ],
        compiler_params=_params(("parallel", "parallel", "parallel")),
        name="cmpattn",
    )(q3, kvc, kvc, mt)


def _selwin_kernel(q_ref, ks_ref, vs_ref, kw_ref, vw_ref, msk_ref, emat_ref, ocmp_ref, gate_ref, o_ref,
                   acc_sc, accw_sc, sa_sc, sb_sc, ba_sc, bb_sc, *, seq):
    assert TQ == TK_SEL, "the window walk assumes key tile n covers the queries of grid step n"
    i = pl.program_id(2)
    tq = TQ
    q = q_ref[0]
    heads = [slice(hh * tq, (hh + 1) * tq) for hh in range(NSA_GROUP)]
    q_t = jnp.concatenate(
        [q[:, hh * HEAD_DIM:(hh + 1) * HEAD_DIM].astype(F32).T.astype(BF16) for hh in range(NSA_GROUP)], axis=1)
    t_lane = i * tq + lax.broadcasted_iota(I32, (1, tq), 1)

    c_exp = SCALE * LOG2_E
    last_tile = seq // TK_SEL - 1

    def make_branch(k_ref, v_ref, acc_ref, bias_of):
        def scores(j, s_ref, b_ref):
            off = pl.multiple_of(j * TK_SEL, TK_SEL)
            s_ref[...] = _dot(k_ref[0, 0, 0, pl.ds(off, TK_SEL), :], q_t)
            kpos = off + lax.broadcasted_iota(I32, (TK_SEL, tq), 0)
            b_ref[...] = jnp.where(bias_of(j, kpos), 0.0, NEG_INF)

        def absorb(j, s_ref, b_ref, m_old):
            bias = b_ref[...]
            m_new, alphas, ps = [], [], []
            for hh in range(NSA_GROUP):
                sh = s_ref[:, heads[hh]] + bias
                mh = jnp.maximum(m_old[hh], jnp.max(sh, axis=0, keepdims=True))
                alphas.append(jnp.exp2((m_old[hh] - mh) * c_exp))
                ps.append(jnp.exp2((sh - mh) * c_exp).astype(BF16))
                m_new.append(mh)
            p_t = jnp.concatenate(ps, axis=1)
            acc_ref[...] = acc_ref[...] * jnp.concatenate(alphas, axis=1) + _dot(v_ref[0, 0, j], p_t)
            return tuple(m_new)

        return scores, absorb

    m0 = tuple(jnp.full((1, tq), NEG_INF, F32) for _ in range(NSA_GROUP))

    selm_t = msk_ref[0, 0]

    def sel_visible(j, kpos):
        picked = _dot(emat_ref[j], selm_t)
        return (picked > 0.5) & (kpos <= t_lane)

    sel_scores, sel_absorb = make_branch(ks_ref, vs_ref, acc_sc, sel_visible)
    acc_sc[...] = jnp.zeros(acc_sc.shape, F32)
    n_kv = (i * tq + tq - 1) // TK_SEL + 1

    sel_scores(0, sa_sc, ba_sc)

    def pair(jj, carry):
        j0 = 2 * jj
        j1 = jnp.minimum(j0 + 1, last_tile)
        sel_scores(j1, sb_sc, bb_sc)
        carry = sel_absorb(j0, sa_sc, ba_sc, carry)
        sel_scores(jnp.minimum(j0 + 2, last_tile), sa_sc, ba_sc)
        return sel_absorb(j1, sb_sc, bb_sc, carry)

    lax.fori_loop(0, (n_kv + 1) >> 1, pair, m0)

    def win_visible(j, kpos):
        return (kpos <= t_lane) & (kpos > t_lane - WINDOW)

    win_scores, win_absorb = make_branch(kw_ref, vw_ref, accw_sc, win_visible)
    accw_sc[...] = jnp.zeros(accw_sc.shape, F32)
    n_back = WINDOW // TK_SEL
    tiles = [i] + [jnp.where(i >= kb, i - kb, kb) for kb in range(1, n_back + 1)]
    bufs = [(sa_sc, ba_sc), (sb_sc, bb_sc)]
    win_scores(tiles[0], *bufs[0])
    m_win = m0
    for n, tile in enumerate(tiles):
        if n + 1 < len(tiles):
            win_scores(tiles[n + 1], *bufs[(n + 1) % 2])
        m_win = win_absorb(tile, *bufs[n % 2], m_win)

    gate = gate_ref[0]
    gate_t = gate.T
    ocmp = ocmp_ref[0]
    for hh in range(NSA_GROUP):
        cols = slice(hh * HEAD_DIM, (hh + 1) * HEAD_DIM)
        o_sel_t = acc_sc[:HEAD_DIM, heads[hh]] * (1.0 / acc_sc[HEAD_DIM:HEAD_DIM + 1, heads[hh]])
        o_win_t = accw_sc[:HEAD_DIM, heads[hh]] * (1.0 / accw_sc[HEAD_DIM:HEAD_DIM + 1, heads[hh]])
        z_t = gate_t[3 * hh + 1:3 * hh + 2, :] * o_sel_t + gate_t[3 * hh + 2:3 * hh + 3, :] * o_win_t
        o = gate[:, 3 * hh:3 * hh + 1] * ocmp[:, cols].astype(F32) + z_t.T
        o_ref[0, :, cols] = o.astype(BF16)


def _selwin(q_rot, kv6, msk, emat, ocmp, gates, b, s):
    q3 = q_rot.reshape(b, s, Q_W)
    g3 = gates.reshape(b, s, NSA_KV_GROUPS * LANES)

    def k_spec(kind):
        return pl.BlockSpec((1, 1, 1, s, HEAD_DIM), lambda bi, g, i: (kind, bi, g, 0, 0))

    def vt_tiles(kind, tile):
        vt = jnp.swapaxes(kv6[kind].reshape(b, NSA_KV_GROUPS, s // tile, tile, HEAD_DIM), -1, -2)
        extra = jnp.zeros((VT_ROWS - HEAD_DIM, tile), BF16).at[0].set(1.0)
        return jnp.concatenate([vt, jnp.broadcast_to(extra, vt.shape[:3] + extra.shape)], axis=3)

    def vt_spec(tile):
        return pl.BlockSpec((1, 1, s // tile, VT_ROWS, tile), lambda bi, g, i: (bi, g, 0, 0, 0))

    return pl.pallas_call(
        functools.partial(_selwin_kernel, seq=s),
        grid=(b, NSA_KV_GROUPS, s // TQ),
        in_specs=[
            pl.BlockSpec((1, TQ, GROUP_W), lambda bi, g, i: (bi, i, g)),
            k_spec(2), vt_spec(TK_SEL), k_spec(4), vt_spec(TQ),
            pl.BlockSpec((1, 1, LANES, TQ), lambda bi, g, i: (bi, g, 0, i)),
            pl.BlockSpec((s // TK_SEL, TK_SEL, LANES), lambda bi, g, i: (0, 0, 0)),
            pl.BlockSpec((1, TQ, GROUP_W), lambda bi, g, i: (bi, i, g)),
            pl.BlockSpec((1, TQ, LANES), lambda bi, g, i: (bi, i, g)),
        ],
        out_specs=pl.BlockSpec((1, TQ, GROUP_W), lambda bi, g, i: (bi, i, g)),
        out_shape=jax.ShapeDtypeStruct((b, s, Q_W), BF16),
        scratch_shapes=[
            pltpu.VMEM((VT_ROWS, NSA_GROUP * TQ), F32),
            pltpu.VMEM((VT_ROWS, NSA_GROUP * TQ), F32),
            pltpu.VMEM((TK_SEL, NSA_GROUP * TQ), F32),
            pltpu.VMEM((TK_SEL, NSA_GROUP * TQ), F32),
            pltpu.VMEM((TK_SEL, TQ), F32),
            pltpu.VMEM((TK_SEL, TQ), F32),
        ],
        compiler_params=_params(("parallel", "parallel", "parallel")),
        name="selwin",
    )(q3, kv6, vt_tiles(3, TK_SEL), kv6, vt_tiles(5, TQ), msk, emat, ocmp, g3)


def _softplus(x):
    return jnp.maximum(x, 0.0) + jnp.log1p(jnp.exp(-jnp.abs(x)))


def _lru_kernel(lx_ref, ly_ref, cw_ref, cb_ref, wa_ref, ba_ref, wi_ref, bi_ref, lam_ref, o_ref, xe_sc, h_sc):
    ts = TS_LRU
    halo = SUBLANES

    @pl.when(pl.program_id(2) == 0)
    def _():
        xe_sc[0:halo, :] = jnp.zeros((halo, CW_LRU), F32)
        h_sc[...] = jnp.zeros(h_sc.shape, F32)

    x = lx_ref[0].astype(F32)
    xe_sc[halo:halo + ts, :] = x
    xc = cb_ref[...]
    for k in range(CONV_WIDTH):
        d = CONV_WIDTH - 1 - k
        xc = xc + xe_sc[pl.ds(halo - d, ts), :] * cw_ref[k:k + 1, :]
    xe_sc[0:halo, :] = x[ts - halo:ts]

    row_in_group = lax.broadcasted_iota(I32, (ts, LRU_BLOCK_W), 0) & (SUBLANES - 1)
    for n in range(CW_LRU // LRU_BLOCK_W):
        cols = slice(n * LRU_BLOCK_W, (n + 1) * LRU_BLOCK_W)
        xcn = xc[:, cols]
        xb = xcn.astype(BF16)
        r = jax.nn.sigmoid(_dot(xb, wa_ref[n]) + ba_ref[:, cols])
        ig = jax.nn.sigmoid(_dot(xb, wi_ref[n]) + bi_ref[:, cols])
        log_a = (-LRU_C * _softplus(-lam_ref[:, cols])) * r
        a = jnp.exp(log_a)
        u = jnp.sqrt(-jnp.tanh(log_a) * (a * a + 1.0)) * (ig * xcn)
        ca, cb = a, u
        for d in (1, 2, 4):
            ok = row_in_group >= d
            ca_sh = pltpu.roll(ca, d, axis=0)
            cb_sh = pltpu.roll(cb, d, axis=0)
            cb = jnp.where(ok, ca * cb_sh + cb, cb)
            ca = jnp.where(ok, ca * ca_sh, ca)
        hprev = h_sc[:, cols]
        hs = []
        for gi in range(ts // SUBLANES):
            rows = slice(gi * SUBLANES, (gi + 1) * SUBLANES)
            hg = ca[rows] * hprev + cb[rows]
            hs.append(hg)
            hprev = hg[SUBLANES - 1:SUBLANES]
        h_sc[:, cols] = hprev
        h = jnp.concatenate(hs, axis=0)
        o_ref[0, :, cols] = (h * jax.nn.gelu(ly_ref[0, :, cols].astype(F32))).astype(BF16)


def _lru(lxy, conv_w, conv_b, wa, ba, wi, bi, lam, b, s):
    lxy3 = lxy.reshape(b, s, 2 * LRU_WIDTH)
    ncw = LRU_WIDTH // CW_LRU
    nb = CW_LRU // LRU_BLOCK_W
    vec = pl.BlockSpec((1, CW_LRU), lambda bi_, c, t: (0, c))
    return pl.pallas_call(
        _lru_kernel,
        grid=(b, ncw, s // TS_LRU),
        in_specs=[
            pl.BlockSpec((1, TS_LRU, CW_LRU), lambda bi_, c, t: (bi_, t, c)),
            pl.BlockSpec((1, TS_LRU, CW_LRU), lambda bi_, c, t: (bi_, t, ncw + c)),
            pl.BlockSpec((CONV_WIDTH, CW_LRU), lambda bi_, c, t: (0, c)),
            vec,
            pl.BlockSpec((nb, LRU_BLOCK_W, LRU_BLOCK_W), lambda bi_, c, t: (c, 0, 0)),
            vec,
            pl.BlockSpec((nb, LRU_BLOCK_W, LRU_BLOCK_W), lambda bi_, c, t: (c, 0, 0)),
            vec,
            vec,
        ],
        out_specs=pl.BlockSpec((1, TS_LRU, CW_LRU), lambda bi_, c, t: (bi_, t, c)),
        out_shape=jax.ShapeDtypeStruct((b, s, LRU_WIDTH), BF16),
        scratch_shapes=[
            pltpu.VMEM((TS_LRU + SUBLANES, CW_LRU), F32),
            pltpu.VMEM((1, CW_LRU), F32),
        ],
        compiler_params=_params(("parallel", "parallel", "arbitrary")),
        name="lru",
    )(lxy3, lxy3, conv_w, conv_b, wa, ba, wi, bi, lam)


def _merge_kernel(h_ref, oa_ref, ob_ref, wga_ref, wgb_ref, wpa_ref, wpb_ref, o_ref):
    h = h_ref[...]
    ga = jax.nn.sigmoid(_dot(h, wga_ref[...]))
    gb = jax.nn.sigmoid(_dot(h, wgb_ref[...]))
    o_ref[...] = (ga * _dot(oa_ref[...], wpa_ref[...]) + gb * _dot(ob_ref[...], wpb_ref[...])).astype(BF16)


def _merge(h, oa, ob, wga, wgb, wpa, wpb):
    t = h.shape[0]
    row = pl.BlockSpec((TM, D_MODEL), lambda j, i: (i, 0))
    col = pl.BlockSpec((D_MODEL, TN), lambda j, i: (0, j))
    return pl.pallas_call(
        _merge_kernel,
        grid=(D_MODEL // TN, t // TM),
        in_specs=[row, row, row, col, col, col, col],
        out_specs=pl.BlockSpec((TM, TN), lambda j, i: (i, j)),
        out_shape=jax.ShapeDtypeStruct((t, D_MODEL), BF16),
        compiler_params=_params(("parallel", "parallel"), VMEM_LIMIT),
        name="merge",
    )(h, oa, ob, wga, wgb, wpa, wpb)


def _outrouter_kernel(mg_ref, x_ref, wo_ref, g2_ref, rw_ref, rb_ref, tri_ref,
                      x1_ref, h2_ref, tope_ref, gate_ref, rank_ref, cnt_ref, carry_sc):
    tm = TM_OUT

    @pl.when(pl.program_id(0) == 0)
    def _():
        carry_sc[...] = jnp.zeros(carry_sc.shape, F32)

    x1 = x_ref[...] + _dot(mg_ref[...], wo_ref[...])
    x1_ref[...] = x1
    y = x1 * lax.rsqrt(jnp.mean(x1 * x1, axis=-1, keepdims=True) + NORM_EPS) * g2_ref[...]
    h2_ref[...] = y
    logits = _dot(y.astype(BF16), rw_ref[...]) + rb_ref[...]
    lane = lax.broadcasted_iota(I32, (tm, LANES), 1)
    lane_f = lane.astype(F32)
    work = jnp.where(lane < N_EXPERTS, logits, -jnp.inf)
    vals, idxs = [], []
    for _k in range(TOP_K):
        m = jnp.max(work, axis=-1, keepdims=True)
        idx = jnp.min(jnp.where(work == m, lane_f, float(LANES)), axis=-1, keepdims=True).astype(I32)
        vals.append(m)
        idxs.append(idx)
        work = jnp.where(lane == idx, -jnp.inf, work)
    es = [jnp.exp(v - vals[0]) for v in vals]
    denom = es[0] + es[1] + es[2] + es[3]
    cnt = jnp.zeros((tm, LANES), F32)
    for idx in idxs:
        cnt = cnt + (lane == idx).astype(F32)
    before = _dot(tri_ref[...], cnt.astype(BF16)) + carry_sc[...]
    tope = jnp.zeros((tm, LANES), I32)
    gate = jnp.zeros((tm, LANES), F32)
    rank = jnp.zeros((tm, LANES), F32)
    for k in range(TOP_K):
        tope = jnp.where(lane == k, idxs[k], tope)
        gate = jnp.where(lane == k, es[k] / denom, gate)
        rk = jnp.sum(jnp.where(lane == idxs[k], before, 0.0), axis=-1, keepdims=True)
        rank = jnp.where(lane == k, rk, rank)
    tope_ref[...] = tope
    gate_ref[...] = gate
    rank_ref[...] = rank.astype(I32)
    carry = carry_sc[...] + jnp.sum(cnt, axis=0, keepdims=True)
    carry_sc[...] = carry
    cnt_ref[...] = jnp.broadcast_to(carry, cnt_ref.shape).astype(I32)


def _outrouter(merged, x2, wo, g2, rw, rb, tri):
    t = x2.shape[0]
    row_bf = pl.BlockSpec((TM_OUT, D_MODEL), lambda i: (i, 0))
    lane_blk = pl.BlockSpec((TM_OUT, LANES), lambda i: (i, 0))
    return pl.pallas_call(
        _outrouter_kernel,
        grid=(t // TM_OUT,),
        in_specs=[
            row_bf,
            row_bf,
            pl.BlockSpec((D_MODEL, D_MODEL), lambda i: (0, 0)),
            pl.BlockSpec((1, D_MODEL), lambda i: (0, 0)),
            pl.BlockSpec((D_MODEL, LANES), lambda i: (0, 0)),
            pl.BlockSpec((1, LANES), lambda i: (0, 0)),
            pl.BlockSpec((TM_OUT, TM_OUT), lambda i: (0, 0)),
        ],
        out_specs=[row_bf, row_bf, lane_blk, lane_blk, lane_blk,
                   pl.BlockSpec((SUBLANES, LANES), lambda i: (0, 0))],
        out_shape=[
            jax.ShapeDtypeStruct((t, D_MODEL), F32),
            jax.ShapeDtypeStruct((t, D_MODEL), F32),
            jax.ShapeDtypeStruct((t, LANES), I32),
            jax.ShapeDtypeStruct((t, LANES), F32),
            jax.ShapeDtypeStruct((t, LANES), I32),
            jax.ShapeDtypeStruct((SUBLANES, LANES), I32),
        ],
        scratch_shapes=[pltpu.VMEM((1, LANES), F32)],
        compiler_params=_params(("arbitrary",), VMEM_LIMIT),
        name="outrouter",
    )(merged, x2, wo, g2, rw, rb, tri)


def _row_copy(src_hbm, dst, src_row, dst_row, sem):
    return pltpu.make_async_copy(src_hbm.at[pl.ds(src_row, 1)], dst.at[pl.ds(dst_row, 1)], sem)


def _issue_rows(n_rows, copy_of_row):
    for r in range(n_rows):
        copy_of_row(r).start()


def _drain_rows(n_rows, any_row_copy):
    def body(r, carry):
        any_row_copy.wait()
        return carry

    lax.fori_loop(0, n_rows, body, 0, unroll=DMA_UNROLL)


def _experts_kernel(ce_ref, na_ref, tokc_ref, tokn_ref, h_hbm, wg_hbm, bg_ref, wu_hbm, bu_ref, wd_hbm, bd_ref,
                    y_ref, xbuf, sem, wg_sc, wu_sc, wd_sc, sg_sc, su_sc, sd_sc, wsem):
    c = pl.program_id(0)
    n_act = na_ref[0]
    slot = c & 1
    last_c = pl.num_programs(0) - 1
    e = ce_ref[c]
    e_next = ce_ref[jnp.minimum(c + 1, last_c)]
    is_first = jnp.logical_or(c == 0, ce_ref[jnp.maximum(c - 1, 0)] != e)
    next_switch = jnp.logical_and(c + 1 < n_act, e_next != e)

    def gather(tok_ref, s_):
        _issue_rows(MOE_ROWS, lambda r: _row_copy(h_hbm, xbuf.at[s_], tok_ref[0, 0, r], r, sem.at[s_]))

    def stage_copies(ex, s, sl):
        span = pl.ds(s * W_STAGE, W_STAGE)
        full = pl.ds(0, D_MODEL)
        return (pltpu.make_async_copy(wg_hbm.at[ex, full, span], sg_sc.at[sl], wsem.at[sl, 0]),
                pltpu.make_async_copy(wu_hbm.at[ex, full, span], su_sc.at[sl], wsem.at[sl, 1]),
                pltpu.make_async_copy(wd_hbm.at[ex, span, full], sd_sc.at[sl], wsem.at[sl, 2]))

    def start_stage(ex, s, sl):
        for cp in stage_copies(ex, s, sl):
            cp.start()

    def prefetch_next():
        for s in range(N_WSLOTS):
            start_stage(e_next, s, s)

    def ffn_chunk(x, acc, lo, width):
        cols = slice(lo, lo + width)
        g = _dot(x, wg_sc[:, cols]) + bg_ref[0, :, cols]
        u = _dot(x, wu_sc[:, cols]) + bu_ref[0, :, cols]
        g = jnp.minimum(g, SWIGLU_LIMIT)
        u = jnp.clip(u, -SWIGLU_LIMIT, SWIGLU_LIMIT)
        act = (u + 1.0) * (g * jax.nn.sigmoid(SWIGLU_ALPHA * g))
        return acc + _dot(act.astype(BF16), wd_sc[cols, :])

    @pl.when(jnp.logical_and(c == 0, n_act > 0))
    def _():
        gather(tokc_ref, 0)
        for s in range(N_WSLOTS):
            start_stage(e, s, s)

    @pl.when(c + 1 < n_act)
    def _():
        gather(tokn_ref, 1 - slot)

    @pl.when(c < n_act)
    def _():
        _drain_rows(MOE_ROWS, _row_copy(h_hbm, xbuf.at[slot], 0, 0, sem.at[slot]))
        x = xbuf[slot].astype(BF16)

        @pl.when(is_first)
        def _():
            acc = jnp.zeros((MOE_ROWS, D_MODEL), F32)
            for s in range(D_FF // W_STAGE):
                sl = s % N_WSLOTS
                span = slice(s * W_STAGE, (s + 1) * W_STAGE)
                for cp in stage_copies(e, s, sl):
                    cp.wait()
                wg_sc[:, span] = sg_sc[sl].astype(BF16)
                wu_sc[:, span] = su_sc[sl].astype(BF16)
                wd_sc[span, :] = sd_sc[sl].astype(BF16)
                if s + N_WSLOTS < D_FF // W_STAGE:
                    start_stage(e, s + N_WSLOTS, sl)
                acc = ffn_chunk(x, acc, s * W_STAGE, W_STAGE)
            y_ref[...] = acc + bd_ref[0]

            @pl.when(next_switch)
            def _():
                prefetch_next()

        @pl.when(jnp.logical_not(is_first))
        def _():
            @pl.when(next_switch)
            def _():
                prefetch_next()

            acc = jnp.zeros((MOE_ROWS, D_MODEL), F32)
            for f in range(D_FF // FF_CHUNK):
                acc = ffn_chunk(x, acc, f * FF_CHUNK, FF_CHUNK)
            y_ref[...] = acc + bd_ref[0]

    @pl.when(c >= n_act)
    def _():
        y_ref[...] = jnp.zeros(y_ref.shape, y_ref.dtype)


def _experts(chunk_e, n_active, row_tok, h2, wg, bg, wu, bu, wd, bd, n_chunks):
    def wspec():
        return pl.BlockSpec(memory_space=pl.ANY)

    def bspec():
        return pl.BlockSpec((1, 1, D_FF), lambda c, ce, na: (ce[c], 0, 0))

    tok3 = row_tok.reshape(n_chunks, 1, MOE_ROWS)
    grid_spec = pltpu.PrefetchScalarGridSpec(
        num_scalar_prefetch=2,
        grid=(n_chunks,),
        in_specs=[
            pl.BlockSpec((1, 1, MOE_ROWS), lambda c, ce, na: (c, 0, 0), memory_space=pltpu.SMEM),
            pl.BlockSpec((1, 1, MOE_ROWS), lambda c, ce, na: (jnp.minimum(c + 1, n_chunks - 1), 0, 0),
                         memory_space=pltpu.SMEM),
            pl.BlockSpec(memory_space=pl.ANY),
            wspec(), bspec(),
            wspec(), bspec(),
            wspec(), bspec(),
        ],
        out_specs=pl.BlockSpec((MOE_ROWS, D_MODEL), lambda c, ce, na: (c, 0)),
        scratch_shapes=[
            pltpu.VMEM((2, MOE_ROWS, D_MODEL), F32),
            pltpu.SemaphoreType.DMA((2,)),
            pltpu.VMEM((D_MODEL, D_FF), BF16),
            pltpu.VMEM((D_MODEL, D_FF), BF16),
            pltpu.VMEM((D_FF, D_MODEL), BF16),
            pltpu.VMEM((N_WSLOTS, D_MODEL, W_STAGE), F32),
            pltpu.VMEM((N_WSLOTS, D_MODEL, W_STAGE), F32),
            pltpu.VMEM((N_WSLOTS, W_STAGE, D_MODEL), F32),
            pltpu.SemaphoreType.DMA((N_WSLOTS, 3)),
        ],
    )
    return pl.pallas_call(
        _experts_kernel,
        grid_spec=grid_spec,
        out_shape=jax.ShapeDtypeStruct((n_chunks * MOE_ROWS, D_MODEL), F32),
        compiler_params=_params(("arbitrary",), VMEM_LIMIT_EXPERTS),
        name="experts",
    )(chunk_e, n_active, tok3, tok3, h2, wg, bg, wu, bu, wd, bd)


def _combine_kernel(destc_ref, destn_ref, gate_ref, x1_ref, gf_ref, y_hbm, o_ref, ybuf, sem):
    i = pl.program_id(0)
    slot = i & 1
    n = TB_COMBINE * TOP_K

    def gather(dest_ref, s_):
        _issue_rows(n, lambda r: _row_copy(y_hbm, ybuf.at[s_, r & (TOP_K - 1)], dest_ref[0, 0, r],
                                           r >> TOP_K_LOG2, sem.at[s_]))

    @pl.when(i == 0)
    def _():
        gather(destc_ref, 0)

    @pl.when(i + 1 < pl.num_programs(0))
    def _():
        gather(destn_ref, 1 - slot)

    _drain_rows(n, _row_copy(y_hbm, ybuf.at[slot, 0], 0, 0, sem.at[slot]))
    gate = gate_ref[...]
    out = x1_ref[...]
    for k in range(TOP_K):
        out = out + gate[:, k:k + 1] * ybuf[slot, k]
    o_ref[...] = out * lax.rsqrt(jnp.mean(out * out, axis=-1, keepdims=True) + NORM_EPS) * gf_ref[...]


def _combine(dest, gate, x1, gf, y_rows):
    t = x1.shape[0]
    nt = t // TB_COMBINE
    dest3 = dest.reshape(nt, 1, TB_COMBINE * TOP_K)
    return pl.pallas_call(
        _combine_kernel,
        grid=(nt,),
        in_specs=[
            pl.BlockSpec((1, 1, TB_COMBINE * TOP_K), lambda i: (i, 0, 0), memory_space=pltpu.SMEM),
            pl.BlockSpec((1, 1, TB_COMBINE * TOP_K), lambda i: (jnp.minimum(i + 1, nt - 1), 0, 0),
                         memory_space=pltpu.SMEM),
            pl.BlockSpec((TB_COMBINE, LANES), lambda i: (i, 0)),
            pl.BlockSpec((TB_COMBINE, D_MODEL), lambda i: (i, 0)),
            pl.BlockSpec((1, D_MODEL), lambda i: (0, 0)),
            pl.BlockSpec(memory_space=pl.ANY),
        ],
        out_specs=pl.BlockSpec((TB_COMBINE, D_MODEL), lambda i: (i, 0)),
        out_shape=jax.ShapeDtypeStruct((t, D_MODEL), F32),
        scratch_shapes=[
            pltpu.VMEM((2, TOP_K, TB_COMBINE, D_MODEL), F32),
            pltpu.SemaphoreType.DMA((2,)),
        ],
        compiler_params=_params(("arbitrary",)),
        name="combine",
    )(dest3, dest3, gate, x1, gf, y_rows)


def _rope_tables(s):
    pos = jnp.arange(s, dtype=F32)
    inv = ROPE_THETA ** (-jnp.arange(0, HEAD_DIM, 2, dtype=F32) / HEAD_DIM)
    ang = pos[:, None] * inv[None, :]
    ang = jnp.concatenate([ang, ang], axis=-1)
    sign = jnp.where(jnp.arange(HEAD_DIM) < HEAD_DIM // 2, -1.0, 1.0).astype(F32)
    return jnp.cos(ang), jnp.sin(ang) * sign[None, :]


def _cmp_to_sel_t(nc, n_sel):
    cs = np.arange(nc) * CMP_STRIDE
    ss = np.arange(n_sel) * SEL_BLOCK
    ov = np.minimum(cs[:, None] + CMP_BLOCK, ss[None, :] + SEL_BLOCK) - np.maximum(cs[:, None], ss[None, :])
    m = np.clip(ov, 0, None) / CMP_BLOCK
    m[nc - 1, :] = 0.0
    out = np.zeros((LANES, nc), np.float32)
    out[:n_sel] = m.T
    return jnp.asarray(out, dtype=BF16)


def _block_expand(s):
    key = np.arange(s).reshape(s // TK_SEL, TK_SEL, 1)
    blk = np.arange(LANES).reshape(1, 1, LANES)
    return jnp.asarray((key // SEL_BLOCK == blk).astype(np.float32), dtype=BF16)


def kernel(x, norm1_g, w_in, cmp_pe_k, cmp_pe_v, cmp_k_w1, cmp_k_w2, cmp_v_w1, cmp_v_w2, lru_conv_w, lru_conv_b, lru_wa, lru_ba, lru_wi, lru_bi, lru_lambda, w_br_attn, w_br_lru, w_out, norm2_g, router_w, router_b, exp_w_gate, exp_b_gate, exp_w_up, exp_b_up, exp_w_down, exp_b_down, normf_g):
    b, s, d = x.shape
    assert d == D_MODEL and norm1_g.shape[0] == 1, "single-layer block of width D_MODEL"
    assert s % TM == 0 and s % TS_LRU == 0 and s >= WIN_KEYS and s // SEL_BLOCK <= LANES
    assert s % (2 * TK_SEL) == 0, "the selected branch walks key tiles in pairs"
    t = b * s
    x2 = x.reshape(t, d)

    w_in0 = w_in[0]
    o = 0
    wq = w_in0[:, o:o + Q_W].astype(BF16); o += Q_W
    wkv = w_in0[:, o:o + KV_KINDS * KV_W].astype(BF16); o += KV_KINDS * KV_W
    n_g = 3 * NSA_Q_HEADS
    wg_nsa = w_in0[:, o:o + n_g].reshape(d, NSA_KV_GROUPS, 3 * NSA_GROUP); o += n_g
    wg_nsa = jnp.pad(wg_nsa, ((0, 0), (0, 0), (0, LANES - 3 * NSA_GROUP))).reshape(d, NSA_KV_GROUPS * LANES).astype(BF16)
    wlxy = w_in0[:, o:o + 2 * LRU_WIDTH].astype(BF16); o += 2 * LRU_WIDTH
    wga = w_in0[:, o:o + D_MODEL].astype(BF16); o += D_MODEL
    wgb = w_in0[:, o:o + D_MODEL].astype(BF16); o += D_MODEL

    cos, sin_signed = _rope_tables(s)
    nc = s // CMP_STRIDE
    n_sel = s // SEL_BLOCK

    h, gates = _norm1(x2, norm1_g, wg_nsa)
    q_raw, q_rot = _qproj(h, wq, cos, sin_signed, s)
    kv6 = _kvproj(h, wkv, cos, sin_signed, b, s)
    lxy = _proj(h, wlxy, "lruproj")

    cw1 = jnp.stack([cmp_k_w1[0], cmp_v_w1[0]]).astype(BF16)
    cw2 = jnp.stack([cmp_k_w2[0], cmp_v_w2[0]]).astype(BF16)
    pe = jnp.stack([cmp_pe_k[0], cmp_pe_v[0]]).reshape(2, 1, CMP_BLOCK * HEAD_DIM)
    pe = jnp.pad(pe, ((0, 0), (0, SUBLANES - 1), (0, 0))).astype(BF16)
    kvc = _cmpkv(kv6, cw1, cw2, pe, b, s)
    ocmp, msk = _cmpattn(q_raw, kvc, _cmp_to_sel_t(nc, n_sel), b, s)
    o_a = _selwin(q_rot, kv6, msk, _block_expand(s), ocmp, gates, b, s)

    o_b = _lru(lxy, lru_conv_w[0], lru_conv_b, lru_wa[0].astype(BF16), lru_ba[0].reshape(1, LRU_WIDTH),
               lru_wi[0].astype(BF16), lru_bi[0].reshape(1, LRU_WIDTH), lru_lambda, b, s)

    merged = _merge(h, o_a.reshape(t, Q_W), o_b.reshape(t, LRU_WIDTH), wga, wgb,
                    w_br_attn[0].astype(BF16), w_br_lru[0].astype(BF16))

    rw = jnp.pad(router_w[0], ((0, 0), (0, LANES - N_EXPERTS))).astype(BF16)
    rb = jnp.pad(router_b, ((0, 0), (0, LANES - N_EXPERTS)))
    tri = jnp.asarray(np.tril(np.ones((TM_OUT, TM_OUT), np.float32), -1), dtype=BF16)
    x1, h2, tope, gate, rank, cnt = _outrouter(merged, x2, w_out[0].astype(BF16), norm2_g, rw, rb, tri)

    counts = cnt[0, :N_EXPERTS]
    padded = (counts + MOE_ROWS - 1) // MOE_ROWS * MOE_ROWS
    pend = jnp.cumsum(padded)
    pstart = pend - padded
    n_rows = t * TOP_K + N_EXPERTS * MOE_ROWS
    n_chunks = n_rows // MOE_ROWS
    dest = (pstart[tope[:, :TOP_K]] + rank[:, :TOP_K]).astype(I32)
    tok_of = jnp.broadcast_to(jnp.arange(t, dtype=I32)[:, None], (t, TOP_K))
    row_tok = jnp.zeros((n_rows,), I32).at[dest.reshape(-1)].set(tok_of.reshape(-1))
    chunk_e = jnp.minimum(jnp.sum((jnp.arange(n_chunks) * MOE_ROWS)[:, None] >= pend[None, :], axis=1),
                          N_EXPERTS - 1).astype(I32)
    n_active = (pend[-1:] // MOE_ROWS).astype(I32)

    y_rows = _experts(chunk_e, n_active, row_tok, h2,
                      exp_w_gate.reshape(N_EXPERTS, D_MODEL, D_FF), exp_b_gate[0].reshape(N_EXPERTS, 1, D_FF),
                      exp_w_up.reshape(N_EXPERTS, D_MODEL, D_FF), exp_b_up[0].reshape(N_EXPERTS, 1, D_FF),
                      exp_w_down.reshape(N_EXPERTS, D_FF, D_MODEL), exp_b_down[0].reshape(N_EXPERTS, 1, D_MODEL),
                      n_chunks)
    out = _combine(dest, gate, x1, normf_g.reshape(1, D_MODEL), y_rows)
    return out.reshape(b, s, d)
```

```python
import functools

import numpy as np
import jax
import jax.numpy as jnp
from jax import lax
from jax.experimental import pallas as pl
from jax.experimental.pallas import tpu as pltpu

F32 = jnp.float32
BF16 = jnp.bfloat16
I32 = jnp.int32

D_MODEL = 2048
HEAD_DIM = 128
NSA_Q_HEADS = 16
NSA_KV_GROUPS = 4
NSA_GROUP = NSA_Q_HEADS // NSA_KV_GROUPS
CMP_BLOCK = 32
CMP_STRIDE = 16
CMP_HIDDEN = 256
SEL_BLOCK = 64
SEL_BLOCK_LOG2 = 6
SEL_TOPK = 16
WINDOW = 512
ROPE_THETA = 10000.0
LRU_WIDTH = 2048
LRU_BLOCKS = 16
LRU_BLOCK_W = LRU_WIDTH // LRU_BLOCKS
CONV_WIDTH = 4
LRU_C = 8.0
N_EXPERTS = 32
TOP_K = 4
TOP_K_LOG2 = 2
D_FF = 2048
SWIGLU_ALPHA = 1.702
SWIGLU_LIMIT = 7.0
NORM_EPS = 1e-6
NEG_INF = -1e30
FORCE_SCORE = 1e6

Q_W = NSA_Q_HEADS * HEAD_DIM
KV_W = NSA_KV_GROUPS * HEAD_DIM
GROUP_W = NSA_GROUP * HEAD_DIM
SCALE = HEAD_DIM ** -0.5
LOG2_E = 1.4426950408889634
LANES = 128
SUBLANES = 8

TM = 1024
TN = 512
TN_WIDE = 1024
TQ_CMP = 256
TQ = 256
TQ_LOG2 = 8
TK_SEL = 256
WIN_KEYS = WINDOW + TQ
VT_ROWS = HEAD_DIM + 16
TS_LRU = 256
CW_LRU = 512
TM_OUT = 256
MOE_ROWS = 256
FF_CHUNK = 512
TB_COMBINE = 128
DMA_UNROLL = 8
W_STAGE = 256
N_WSLOTS = 2
VMEM_LIMIT = 56 * 1024 * 1024
VMEM_LIMIT_EXPERTS = 60 * 1024 * 1024


def _params(sem, vmem=None):
    return pltpu.CompilerParams(dimension_semantics=sem, vmem_limit_bytes=vmem)


def _dot(a, b):
    return jnp.dot(a, b, preferred_element_type=F32)


def _dot_nt(a, b, precision=None):
    return lax.dot_general(a, b, (((1,), (1,)), ((), ())), precision=precision, preferred_element_type=F32)


def _norm1_kernel(x_ref, g_ref, wg_ref, h_ref, gate_ref):
    x = x_ref[...]
    y = x * lax.rsqrt(jnp.mean(x * x, axis=-1, keepdims=True) + NORM_EPS) * g_ref[...]
    hb = y.astype(BF16)
    h_ref[...] = hb
    gate_ref[...] = jax.nn.sigmoid(_dot(hb, wg_ref[...]))


def _norm1(x2, g, wg):
    t = x2.shape[0]
    return pl.pallas_call(
        _norm1_kernel,
        grid=(t // TM,),
        in_specs=[
            pl.BlockSpec((TM, D_MODEL), lambda i: (i, 0)),
            pl.BlockSpec((1, D_MODEL), lambda i: (0, 0)),
            pl.BlockSpec((D_MODEL, NSA_KV_GROUPS * LANES), lambda i: (0, 0)),
        ],
        out_specs=[
            pl.BlockSpec((TM, D_MODEL), lambda i: (i, 0)),
            pl.BlockSpec((TM, NSA_KV_GROUPS * LANES), lambda i: (i, 0)),
        ],
        out_shape=[
            jax.ShapeDtypeStruct((t, D_MODEL), BF16),
            jax.ShapeDtypeStruct((t, NSA_KV_GROUPS * LANES), F32),
        ],
        compiler_params=_params(("parallel",)),
        name="norm1",
    )(x2, g, wg)


def _rope(x, cos, sin_signed):
    return x * cos + pltpu.roll(x, HEAD_DIM // 2, axis=1) * sin_signed


def _qproj_kernel(h_ref, w_ref, cos_ref, sin_ref, qraw_ref, qrot_ref):
    acc = _dot(h_ref[...], w_ref[...])
    qraw_ref[...] = acc.astype(BF16)
    cos = cos_ref[...]
    sin = sin_ref[...]
    for hh in range(TN_WIDE // HEAD_DIM):
        sl = slice(hh * HEAD_DIM, (hh + 1) * HEAD_DIM)
        qrot_ref[:, sl] = _rope(acc[:, sl], cos, sin).astype(BF16)


def _qproj(h, wq, cos, sin, s):
    t = h.shape[0]
    ns = s // TM
    return pl.pallas_call(
        _qproj_kernel,
        grid=(Q_W // TN_WIDE, t // TM),
        in_specs=[
            pl.BlockSpec((TM, D_MODEL), lambda j, i: (i, 0)),
            pl.BlockSpec((D_MODEL, TN_WIDE), lambda j, i: (0, j)),
            pl.BlockSpec((TM, HEAD_DIM), lambda j, i: (i % ns, 0)),
            pl.BlockSpec((TM, HEAD_DIM), lambda j, i: (i % ns, 0)),
        ],
        out_specs=[
            pl.BlockSpec((TM, TN_WIDE), lambda j, i: (i, j)),
            pl.BlockSpec((TM, TN_WIDE), lambda j, i: (i, j)),
        ],
        out_shape=[jax.ShapeDtypeStruct((t, Q_W), BF16), jax.ShapeDtypeStruct((t, Q_W), BF16)],
        compiler_params=_params(("parallel", "parallel")),
        name="qproj",
    )(h, wq, cos, sin)


KV_KINDS = 6
ROW_KINDS = 4
ROW_K_SEL = 2
ROW_K_WIN = 3


def _kvproj_kernel(h_ref, w_ref, cos_ref, sin_ref, out_ref):
    kind = pl.program_id(0)
    acc = _dot(h_ref[...], w_ref[...])
    cos = cos_ref[...]
    sin = sin_ref[...]
    is_rope = jnp.logical_or(kind == ROW_K_SEL, kind == ROW_K_WIN)
    for g in range(NSA_KV_GROUPS):
        xh = acc[:, g * HEAD_DIM:(g + 1) * HEAD_DIM]
        out_ref[0, 0, g] = jnp.where(is_rope, _rope(xh, cos, sin), xh).astype(BF16)


def _kvproj(h, wk4, cos, sin, b, s):
    t = h.shape[0]
    ns = s // TM
    return pl.pallas_call(
        _kvproj_kernel,
        grid=(ROW_KINDS, t // TM),
        in_specs=[
            pl.BlockSpec((TM, D_MODEL), lambda k, i: (i, 0)),
            pl.BlockSpec((D_MODEL, KV_W), lambda k, i: (0, k)),
            pl.BlockSpec((TM, HEAD_DIM), lambda k, i: (i % ns, 0)),
            pl.BlockSpec((TM, HEAD_DIM), lambda k, i: (i % ns, 0)),
        ],
        out_specs=pl.BlockSpec((1, 1, NSA_KV_GROUPS, TM, HEAD_DIM), lambda k, i: (k, i // ns, 0, i % ns, 0)),
        out_shape=jax.ShapeDtypeStruct((ROW_KINDS, b, NSA_KV_GROUPS, s, HEAD_DIM), BF16),
        compiler_params=_params(("parallel", "parallel")),
        name="kvproj",
    )(h, wk4, cos, sin)


def _vtproj_kernel(h_ref, w_ref, out_ref):
    acc = _dot(h_ref[...], w_ref[...])
    sub = lax.broadcasted_iota(I32, (VT_ROWS - HEAD_DIM, TK_SEL), 0)
    ones_row = jnp.where(sub == 0, 1.0, 0.0).astype(BF16)
    for g in range(NSA_KV_GROUPS):
        for tt in range(TM // TK_SEL):
            tile = acc[tt * TK_SEL:(tt + 1) * TK_SEL, g * HEAD_DIM:(g + 1) * HEAD_DIM]
            out_ref[0, 0, g, tt, :HEAD_DIM, :] = tile.T.astype(BF16)
            out_ref[0, 0, g, tt, HEAD_DIM:, :] = ones_row


def _vtproj(h, wv2, b, s):
    t = h.shape[0]
    ns = s // TM
    nt = TM // TK_SEL
    return pl.pallas_call(
        _vtproj_kernel,
        grid=(2, t // TM),
        in_specs=[
            pl.BlockSpec((TM, D_MODEL), lambda k, i: (i, 0)),
            pl.BlockSpec((D_MODEL, KV_W), lambda k, i: (0, k)),
        ],
        out_specs=pl.BlockSpec((1, 1, NSA_KV_GROUPS, nt, VT_ROWS, TK_SEL),
                               lambda k, i: (k, i // ns, 0, i % ns, 0, 0)),
        out_shape=jax.ShapeDtypeStruct((2, b, NSA_KV_GROUPS, s // TK_SEL, VT_ROWS, TK_SEL), BF16),
        compiler_params=_params(("parallel", "parallel")),
        name="vtproj",
    )(h, wv2)


def _proj_kernel(h_ref, w_ref, o_ref):
    o_ref[...] = _dot(h_ref[...], w_ref[...]).astype(o_ref.dtype)


def _proj(h, w, name):
    t = h.shape[0]
    n = w.shape[1]
    return pl.pallas_call(
        _proj_kernel,
        grid=(n // TN_WIDE, t // TM),
        in_specs=[
            pl.BlockSpec((TM, D_MODEL), lambda j, i: (i, 0)),
            pl.BlockSpec((D_MODEL, TN_WIDE), lambda j, i: (0, j)),
        ],
        out_specs=pl.BlockSpec((TM, TN_WIDE), lambda j, i: (i, j)),
        out_shape=jax.ShapeDtypeStruct((t, n), BF16),
        compiler_params=_params(("parallel", "parallel")),
        name=name,
    )(h, w)


def _cmpkv_kernel(c_ref, w1_ref, w2_ref, pe_ref, o_ref):
    c = c_ref[0, 0, 0]
    half = CMP_STRIDE * HEAD_DIM
    top = _dot(c, w1_ref[0, :half, :])
    bot = _dot(c, w1_ref[0, half:, :])
    n_rows = top.shape[0]
    bot_next = pltpu.roll(bot, n_rows - 1, axis=0)
    bias = _dot(pe_ref[0], w1_ref[0])[0:1, :]
    hid = jax.nn.gelu(top + bot_next + bias)
    o_ref[0, 0, 0] = _dot(hid.astype(BF16), w2_ref[0]).astype(BF16)


def _cmpkv(kv4, w1, w2, pe, b, s):
    nc = s // CMP_STRIDE
    kvc = kv4[:2].reshape(2, b, NSA_KV_GROUPS, nc, CMP_STRIDE * HEAD_DIM)
    return pl.pallas_call(
        _cmpkv_kernel,
        grid=(2, b, NSA_KV_GROUPS),
        in_specs=[
            pl.BlockSpec((1, 1, 1, nc, CMP_STRIDE * HEAD_DIM), lambda k, bi, g: (k, bi, g, 0, 0)),
            pl.BlockSpec((1, CMP_BLOCK * HEAD_DIM, CMP_HIDDEN), lambda k, bi, g: (k, 0, 0)),
            pl.BlockSpec((1, CMP_HIDDEN, HEAD_DIM), lambda k, bi, g: (k, 0, 0)),
            pl.BlockSpec((1, SUBLANES, CMP_BLOCK * HEAD_DIM), lambda k, bi, g: (k, 0, 0)),
        ],
        out_specs=pl.BlockSpec((1, 1, 1, nc, HEAD_DIM), lambda k, bi, g: (k, bi, g, 0, 0)),
        out_shape=jax.ShapeDtypeStruct((2, b, NSA_KV_GROUPS, nc, HEAD_DIM), BF16),
        compiler_params=_params(("parallel", "parallel", "parallel")),
        name="cmpkv",
    )(kvc, w1, w2, pe)


def _cmpattn_kernel(q_ref, kc_ref, vc_ref, mt_ref, ocmp_ref, msk_ref, *, n_sel):
    i = pl.program_id(2)
    tq = TQ_CMP
    q = q_ref[0]
    kc = kc_ref[0, 0, 0]
    vc = vc_ref[0, 0, 0]
    nc = kc.shape[0]
    t_row = i * tq + lax.broadcasted_iota(I32, (tq, 1), 0)
    n_lane = lax.broadcasted_iota(I32, (tq, nc), 1)
    cmask = (n_lane * CMP_STRIDE + (CMP_BLOCK - 1)) <= t_row
    psum = jnp.zeros((tq, nc), F32)
    for hh in range(NSA_GROUP):
        sl = slice(hh * HEAD_DIM, (hh + 1) * HEAD_DIM)
        s = jnp.where(cmask, _dot_nt(q[:, sl], kc), NEG_INF)
        e = jnp.exp2((s - jnp.max(s, axis=-1, keepdims=True)) * (SCALE * LOG2_E))
        p = jnp.where(cmask, e * (1.0 / jnp.sum(e, axis=-1, keepdims=True)), 0.0)
        ocmp_ref[0, :, sl] = _dot(p.astype(BF16), vc).astype(BF16)
        psum = psum + p
    imp_t = _dot_nt(mt_ref[...], psum.astype(BF16))
    imp_t = imp_t[:n_sel]
    jb = lax.broadcasted_iota(I32, (n_sel, tq), 0)
    cur = (i * tq + lax.broadcasted_iota(I32, (n_sel, tq), 1)) >> SEL_BLOCK_LOG2
    forced = (jb == 0) | (jb == cur) | (jb == cur - 1)
    valid = jb <= cur
    score = jnp.where(forced, FORCE_SCORE, jnp.where(valid, imp_t, NEG_INF))
    pad = jnp.zeros((LANES - n_sel, tq), F32)
    needs_rank = (i + 1) * tq > SEL_TOPK * SEL_BLOCK

    @pl.when(needs_rank)
    def _():
        n_grp = n_sel // SUBLANES
        grp = [score[gi * SUBLANES:(gi + 1) * SUBLANES] for gi in range(n_grp)]
        jb_grp = lax.broadcasted_iota(I32, (SUBLANES, tq), 0)
        ranks = [jnp.zeros((SUBLANES, tq), F32) for _ in range(n_grp)]
        for jp in range(n_sel):
            row = score[jp:jp + 1, :]
            for gi in range(n_grp):
                lo = gi * SUBLANES
                if lo > jp:
                    beats = row >= grp[gi]
                elif lo + SUBLANES - 1 <= jp:
                    beats = row > grp[gi]
                else:
                    beats = (row > grp[gi]) | ((row == grp[gi]) & (jb_grp > jp - lo))
                ranks[gi] = ranks[gi] + jnp.where(beats, 1.0, 0.0)
        rank = jnp.concatenate(ranks, axis=0)
        member = (valid & (rank < float(SEL_TOPK))).astype(F32)
        msk_ref[0, 0] = jnp.concatenate([member, pad], axis=0).astype(BF16)

    @pl.when(jnp.logical_not(needs_rank))
    def _():
        msk_ref[0, 0] = jnp.concatenate([valid.astype(F32), pad], axis=0).astype(BF16)


def _cmpattn(q_raw, kvc, mt, b, s):
    nc = s // CMP_STRIDE
    n_sel = s // SEL_BLOCK
    q3 = q_raw.reshape(b, s, Q_W)
    return pl.pallas_call(
        functools.partial(_cmpattn_kernel, n_sel=n_sel),
        grid=(b, NSA_KV_GROUPS, s // TQ_CMP),
        in_specs=[
            pl.BlockSpec((1, TQ_CMP, GROUP_W), lambda bi, g, i: (bi, i, g)),
            pl.BlockSpec((1, 1, 1, nc, HEAD_DIM), lambda bi, g, i: (0, bi, g, 0, 0)),
            pl.BlockSpec((1, 1, 1, nc, HEAD_DIM), lambda bi, g, i: (1, bi, g, 0, 0)),
            pl.BlockSpec((LANES, nc), lambda bi, g, i: (0, 0)),
        ],
        out_specs=[
            pl.BlockSpec((1, TQ_CMP, GROUP_W), lambda bi, g, i: (bi, i, g)),
            pl.BlockSpec((1, 1, LANES, TQ_CMP), lambda bi, g, i: (bi, g, 0, i)),
        ],
        out_shape=[
            jax.ShapeDtypeStruct((b, s, Q_W), BF16),
            jax.ShapeDtypeStruct((b, NSA_KV_GROUPS, LANES, s), BF16),
        ],
        compiler_params=_params(("parallel", "parallel", "parallel")),
        name="cmpattn",
    )(q3, kvc, kvc, mt)


def _selwin_kernel(q_ref, ks_ref, vs_ref, kw_ref, vw_ref, msk_ref, emat_ref, ocmp_ref, gate_ref, o_ref,
                   acc_sc, accw_sc, sa_sc, sb_sc, ba_sc, bb_sc, *, seq):
    assert TQ == TK_SEL, "the window walk assumes key tile n covers the queries of grid step n"
    i = pl.program_id(2)
    tq = TQ
    q = q_ref[0]
    heads = [slice(hh * tq, (hh + 1) * tq) for hh in range(NSA_GROUP)]
    q_t = jnp.concatenate(
        [q[:, hh * HEAD_DIM:(hh + 1) * HEAD_DIM].astype(F32).T.astype(BF16) for hh in range(NSA_GROUP)], axis=1)
    t_lane = i * tq + lax.broadcasted_iota(I32, (1, tq), 1)

    c_exp = SCALE * LOG2_E
    last_tile = seq // TK_SEL - 1

    def make_branch(k_ref, v_ref, acc_ref, bias_of):
        def scores(j, s_ref, b_ref):
            off = pl.multiple_of(j * TK_SEL, TK_SEL)
            s_ref[...] = _dot(k_ref[0, 0, 0, pl.ds(off, TK_SEL), :], q_t)
            kpos = off + lax.broadcasted_iota(I32, (TK_SEL, tq), 0)
            b_ref[...] = jnp.where(bias_of(j, kpos), 0.0, NEG_INF)

        def absorb(j, s_ref, b_ref, m_old):
            bias = b_ref[...]
            m_new, alphas, ps = [], [], []
            for hh in range(NSA_GROUP):
                sh = s_ref[:, heads[hh]] + bias
                mh = jnp.maximum(m_old[hh], jnp.max(sh, axis=0, keepdims=True))
                alphas.append(jnp.exp2((m_old[hh] - mh) * c_exp))
                ps.append(jnp.exp2((sh - mh) * c_exp).astype(BF16))
                m_new.append(mh)
            p_t = jnp.concatenate(ps, axis=1)
            acc_ref[...] = acc_ref[...] * jnp.concatenate(alphas, axis=1) + _dot(v_ref[0, 0, 0, j], p_t)
            return tuple(m_new)

        return scores, absorb

    m0 = tuple(jnp.full((1, tq), NEG_INF, F32) for _ in range(NSA_GROUP))

    selm_t = msk_ref[0, 0]

    def sel_visible(j, kpos):
        picked = _dot(emat_ref[j], selm_t)
        return (picked > 0.5) & (kpos <= t_lane)

    sel_scores, sel_absorb = make_branch(ks_ref, vs_ref, acc_sc, sel_visible)
    acc_sc[...] = jnp.zeros(acc_sc.shape, F32)
    n_kv = (i * tq + tq - 1) // TK_SEL + 1

    sel_scores(0, sa_sc, ba_sc)

    def pair(jj, carry):
        j0 = 2 * jj
        j1 = jnp.minimum(j0 + 1, last_tile)
        sel_scores(j1, sb_sc, bb_sc)
        carry = sel_absorb(j0, sa_sc, ba_sc, carry)
        sel_scores(jnp.minimum(j0 + 2, last_tile), sa_sc, ba_sc)
        return sel_absorb(j1, sb_sc, bb_sc, carry)

    lax.fori_loop(0, (n_kv + 1) >> 1, pair, m0)

    def win_visible(j, kpos):
        return (kpos <= t_lane) & (kpos > t_lane - WINDOW)

    win_scores, win_absorb = make_branch(kw_ref, vw_ref, accw_sc, win_visible)
    accw_sc[...] = jnp.zeros(accw_sc.shape, F32)
    n_back = WINDOW // TK_SEL
    tiles = [i] + [jnp.where(i >= kb, i - kb, kb) for kb in range(1, n_back + 1)]
    bufs = [(sa_sc, ba_sc), (sb_sc, bb_sc)]
    win_scores(tiles[0], *bufs[0])
    m_win = m0
    for n, tile in enumerate(tiles):
        if n + 1 < len(tiles):
            win_scores(tiles[n + 1], *bufs[(n + 1) % 2])
        m_win = win_absorb(tile, *bufs[n % 2], m_win)

    gate = gate_ref[0]
    gate_t = gate.T
    ocmp = ocmp_ref[0]
    for hh in range(NSA_GROUP):
        cols = slice(hh * HEAD_DIM, (hh + 1) * HEAD_DIM)
        o_sel_t = acc_sc[:HEAD_DIM, heads[hh]] * (1.0 / acc_sc[HEAD_DIM:HEAD_DIM + 1, heads[hh]])
        o_win_t = accw_sc[:HEAD_DIM, heads[hh]] * (1.0 / accw_sc[HEAD_DIM:HEAD_DIM + 1, heads[hh]])
        z_t = gate_t[3 * hh + 1:3 * hh + 2, :] * o_sel_t + gate_t[3 * hh + 2:3 * hh + 3, :] * o_win_t
        o = gate[:, 3 * hh:3 * hh + 1] * ocmp[:, cols].astype(F32) + z_t.T
        o_ref[0, :, cols] = o.astype(BF16)


def _selwin(q_rot, kv4, vt2, msk, emat, ocmp, gates, b, s):
    q3 = q_rot.reshape(b, s, Q_W)
    g3 = gates.reshape(b, s, NSA_KV_GROUPS * LANES)

    def k_spec(kind):
        return pl.BlockSpec((1, 1, 1, s, HEAD_DIM), lambda bi, g, i: (kind, bi, g, 0, 0))

    def vt_spec(kind):
        return pl.BlockSpec((1, 1, 1, s // TK_SEL, VT_ROWS, TK_SEL), lambda bi, g, i: (kind, bi, g, 0, 0, 0))

    return pl.pallas_call(
        functools.partial(_selwin_kernel, seq=s),
        grid=(b, NSA_KV_GROUPS, s // TQ),
        in_specs=[
            pl.BlockSpec((1, TQ, GROUP_W), lambda bi, g, i: (bi, i, g)),
            k_spec(ROW_K_SEL), vt_spec(0), k_spec(ROW_K_WIN), vt_spec(1),
            pl.BlockSpec((1, 1, LANES, TQ), lambda bi, g, i: (bi, g, 0, i)),
            pl.BlockSpec((s // TK_SEL, TK_SEL, LANES), lambda bi, g, i: (0, 0, 0)),
            pl.BlockSpec((1, TQ, GROUP_W), lambda bi, g, i: (bi, i, g)),
            pl.BlockSpec((1, TQ, LANES), lambda bi, g, i: (bi, i, g)),
        ],
        out_specs=pl.BlockSpec((1, TQ, GROUP_W), lambda bi, g, i: (bi, i, g)),
        out_shape=jax.ShapeDtypeStruct((b, s, Q_W), BF16),
        scratch_shapes=[
            pltpu.VMEM((VT_ROWS, NSA_GROUP * TQ), F32),
            pltpu.VMEM((VT_ROWS, NSA_GROUP * TQ), F32),
            pltpu.VMEM((TK_SEL, NSA_GROUP * TQ), F32),
            pltpu.VMEM((TK_SEL, NSA_GROUP * TQ), F32),
            pltpu.VMEM((TK_SEL, TQ), F32),
            pltpu.VMEM((TK_SEL, TQ), F32),
        ],
        compiler_params=_params(("parallel", "parallel", "parallel")),
        name="selwin",
    )(q3, kv4, vt2, kv4, vt2, msk, emat, ocmp, g3)


def _softplus(x):
    return jnp.maximum(x, 0.0) + jnp.log1p(jnp.exp(-jnp.abs(x)))


def _lru_kernel(lx_ref, ly_ref, cw_ref, cb_ref, wa_ref, ba_ref, wi_ref, bi_ref, lam_ref, o_ref, xe_sc, h_sc):
    ts = TS_LRU
    halo = SUBLANES

    @pl.when(pl.program_id(2) == 0)
    def _():
        xe_sc[0:halo, :] = jnp.zeros((halo, CW_LRU), F32)
        h_sc[...] = jnp.zeros(h_sc.shape, F32)

    x = lx_ref[0].astype(F32)
    xe_sc[halo:halo + ts, :] = x
    xc = cb_ref[...]
    for k in range(CONV_WIDTH):
        d = CONV_WIDTH - 1 - k
        xc = xc + xe_sc[pl.ds(halo - d, ts), :] * cw_ref[k:k + 1, :]
    xe_sc[0:halo, :] = x[ts - halo:ts]

    row_in_group = lax.broadcasted_iota(I32, (ts, LRU_BLOCK_W), 0) & (SUBLANES - 1)
    for n in range(CW_LRU // LRU_BLOCK_W):
        cols = slice(n * LRU_BLOCK_W, (n + 1) * LRU_BLOCK_W)
        xcn = xc[:, cols]
        xb = xcn.astype(BF16)
        r = jax.nn.sigmoid(_dot(xb, wa_ref[n]) + ba_ref[:, cols])
        ig = jax.nn.sigmoid(_dot(xb, wi_ref[n]) + bi_ref[:, cols])
        log_a = (-LRU_C * _softplus(-lam_ref[:, cols])) * r
        a = jnp.exp(log_a)
        u = jnp.sqrt(-jnp.tanh(log_a) * (a * a + 1.0)) * (ig * xcn)
        ca, cb = a, u
        for d in (1, 2, 4):
            ok = row_in_group >= d
            ca_sh = pltpu.roll(ca, d, axis=0)
            cb_sh = pltpu.roll(cb, d, axis=0)
            cb = jnp.where(ok, ca * cb_sh + cb, cb)
            ca = jnp.where(ok, ca * ca_sh, ca)
        hprev = h_sc[:, cols]
        hs = []
        for gi in range(ts // SUBLANES):
            rows = slice(gi * SUBLANES, (gi + 1) * SUBLANES)
            hg = ca[rows] * hprev + cb[rows]
            hs.append(hg)
            hprev = hg[SUBLANES - 1:SUBLANES]
        h_sc[:, cols] = hprev
        h = jnp.concatenate(hs, axis=0)
        o_ref[0, :, cols] = (h * jax.nn.gelu(ly_ref[0, :, cols].astype(F32))).astype(BF16)


def _lru(lxy, conv_w, conv_b, wa, ba, wi, bi, lam, b, s):
    lxy3 = lxy.reshape(b, s, 2 * LRU_WIDTH)
    ncw = LRU_WIDTH // CW_LRU
    nb = CW_LRU // LRU_BLOCK_W
    vec = pl.BlockSpec((1, CW_LRU), lambda bi_, c, t: (0, c))
    return pl.pallas_call(
        _lru_kernel,
        grid=(b, ncw, s // TS_LRU),
        in_specs=[
            pl.BlockSpec((1, TS_LRU, CW_LRU), lambda bi_, c, t: (bi_, t, c)),
            pl.BlockSpec((1, TS_LRU, CW_LRU), lambda bi_, c, t: (bi_, t, ncw + c)),
            pl.BlockSpec((CONV_WIDTH, CW_LRU), lambda bi_, c, t: (0, c)),
            vec,
            pl.BlockSpec((nb, LRU_BLOCK_W, LRU_BLOCK_W), lambda bi_, c, t: (c, 0, 0)),
            vec,
            pl.BlockSpec((nb, LRU_BLOCK_W, LRU_BLOCK_W), lambda bi_, c, t: (c, 0, 0)),
            vec,
            vec,
        ],
        out_specs=pl.BlockSpec((1, TS_LRU, CW_LRU), lambda bi_, c, t: (bi_, t, c)),
        out_shape=jax.ShapeDtypeStruct((b, s, LRU_WIDTH), BF16),
        scratch_shapes=[
            pltpu.VMEM((TS_LRU + SUBLANES, CW_LRU), F32),
            pltpu.VMEM((1, CW_LRU), F32),
        ],
        compiler_params=_params(("parallel", "parallel", "arbitrary")),
        name="lru",
    )(lxy3, lxy3, conv_w, conv_b, wa, ba, wi, bi, lam)


def _merge_kernel(h_ref, oa_ref, ob_ref, wga_ref, wgb_ref, wpa_ref, wpb_ref, o_ref):
    h = h_ref[...]
    ga = jax.nn.sigmoid(_dot(h, wga_ref[...]))
    gb = jax.nn.sigmoid(_dot(h, wgb_ref[...]))
    o_ref[...] = (ga * _dot(oa_ref[...], wpa_ref[...]) + gb * _dot(ob_ref[...], wpb_ref[...])).astype(BF16)


def _merge(h, oa, ob, wga, wgb, wpa, wpb):
    t = h.shape[0]
    row = pl.BlockSpec((TM, D_MODEL), lambda j, i: (i, 0))
    col = pl.BlockSpec((D_MODEL, TN), lambda j, i: (0, j))
    return pl.pallas_call(
        _merge_kernel,
        grid=(D_MODEL // TN, t // TM),
        in_specs=[row, row, row, col, col, col, col],
        out_specs=pl.BlockSpec((TM, TN), lambda j, i: (i, j)),
        out_shape=jax.ShapeDtypeStruct((t, D_MODEL), BF16),
        compiler_params=_params(("parallel", "parallel"), VMEM_LIMIT),
        name="merge",
    )(h, oa, ob, wga, wgb, wpa, wpb)


def _outrouter_kernel(mg_ref, x_ref, wo_ref, g2_ref, rw_ref, rb_ref, tri_ref,
                      x1_ref, h2_ref, tope_ref, gate_ref, rank_ref, cnt_ref, carry_sc):
    tm = TM_OUT

    @pl.when(pl.program_id(0) == 0)
    def _():
        carry_sc[...] = jnp.zeros(carry_sc.shape, F32)

    x1 = x_ref[...] + _dot(mg_ref[...], wo_ref[...])
    x1_ref[...] = x1
    y = x1 * lax.rsqrt(jnp.mean(x1 * x1, axis=-1, keepdims=True) + NORM_EPS) * g2_ref[...]
    h2_ref[...] = y
    logits = _dot(y.astype(BF16), rw_ref[...]) + rb_ref[...]
    lane = lax.broadcasted_iota(I32, (tm, LANES), 1)
    lane_f = lane.astype(F32)
    work = jnp.where(lane < N_EXPERTS, logits, -jnp.inf)
    vals, idxs = [], []
    for _k in range(TOP_K):
        m = jnp.max(work, axis=-1, keepdims=True)
        idx = jnp.min(jnp.where(work == m, lane_f, float(LANES)), axis=-1, keepdims=True).astype(I32)
        vals.append(m)
        idxs.append(idx)
        work = jnp.where(lane == idx, -jnp.inf, work)
    es = [jnp.exp(v - vals[0]) for v in vals]
    denom = es[0] + es[1] + es[2] + es[3]
    cnt = jnp.zeros((tm, LANES), F32)
    for idx in idxs:
        cnt = cnt + (lane == idx).astype(F32)
    before = _dot(tri_ref[...], cnt.astype(BF16)) + carry_sc[...]
    tope = jnp.zeros((tm, LANES), I32)
    gate = jnp.zeros((tm, LANES), F32)
    rank = jnp.zeros((tm, LANES), F32)
    for k in range(TOP_K):
        tope = jnp.where(lane == k, idxs[k], tope)
        gate = jnp.where(lane == k, es[k] / denom, gate)
        rk = jnp.sum(jnp.where(lane == idxs[k], before, 0.0), axis=-1, keepdims=True)
        rank = jnp.where(lane == k, rk, rank)
    tope_ref[...] = tope
    gate_ref[...] = gate
    rank_ref[...] = rank.astype(I32)
    carry = carry_sc[...] + jnp.sum(cnt, axis=0, keepdims=True)
    carry_sc[...] = carry
    cnt_ref[...] = jnp.broadcast_to(carry, cnt_ref.shape).astype(I32)


def _outrouter(merged, x2, wo, g2, rw, rb, tri):
    t = x2.shape[0]
    row_bf = pl.BlockSpec((TM_OUT, D_MODEL), lambda i: (i, 0))
    lane_blk = pl.BlockSpec((TM_OUT, LANES), lambda i: (i, 0))
    return pl.pallas_call(
        _outrouter_kernel,
        grid=(t // TM_OUT,),
        in_specs=[
            row_bf,
            row_bf,
            pl.BlockSpec((D_MODEL, D_MODEL), lambda i: (0, 0)),
            pl.BlockSpec((1, D_MODEL), lambda i: (0, 0)),
            pl.BlockSpec((D_MODEL, LANES), lambda i: (0, 0)),
            pl.BlockSpec((1, LANES), lambda i: (0, 0)),
            pl.BlockSpec((TM_OUT, TM_OUT), lambda i: (0, 0)),
        ],
        out_specs=[row_bf, row_bf, lane_blk, lane_blk, lane_blk,
                   pl.BlockSpec((SUBLANES, LANES), lambda i: (0, 0))],
        out_shape=[
            jax.ShapeDtypeStruct((t, D_MODEL), F32),
            jax.ShapeDtypeStruct((t, D_MODEL), F32),
            jax.ShapeDtypeStruct((t, LANES), I32),
            jax.ShapeDtypeStruct((t, LANES), F32),
            jax.ShapeDtypeStruct((t, LANES), I32),
            jax.ShapeDtypeStruct((SUBLANES, LANES), I32),
        ],
        scratch_shapes=[pltpu.VMEM((1, LANES), F32)],
        compiler_params=_params(("arbitrary",), VMEM_LIMIT),
        name="outrouter",
    )(merged, x2, wo, g2, rw, rb, tri)


def _row_copy(src_hbm, dst, src_row, dst_row, sem):
    return pltpu.make_async_copy(src_hbm.at[pl.ds(src_row, 1)], dst.at[pl.ds(dst_row, 1)], sem)


def _issue_rows(n_rows, copy_of_row, n_queues=1):
    for r in range(n_rows):
        copy_of_row(r).start(priority=r % n_queues)


def _drain_rows(n_rows, any_row_copy):
    def body(r, carry):
        any_row_copy.wait()
        return carry

    lax.fori_loop(0, n_rows, body, 0, unroll=DMA_UNROLL)


def _experts_kernel(ce_ref, na_ref, tokc_ref, tokn_ref, h_hbm, wg_hbm, bg_ref, wu_hbm, bu_ref, wd_hbm, bd_ref,
                    y_ref, xbuf, sem, wg_sc, wu_sc, wd_sc, sg_sc, su_sc, sd_sc, wsem):
    c = pl.program_id(0)
    n_act = na_ref[0]
    slot = c & 1
    last_c = pl.num_programs(0) - 1
    e = ce_ref[c]
    e_next = ce_ref[jnp.minimum(c + 1, last_c)]
    is_first = jnp.logical_or(c == 0, ce_ref[jnp.maximum(c - 1, 0)] != e)
    next_switch = jnp.logical_and(c + 1 < n_act, e_next != e)

    def gather(tok_ref, s_):
        _issue_rows(MOE_ROWS, lambda r: _row_copy(h_hbm, xbuf.at[s_], tok_ref[0, 0, r], r, sem.at[s_]))

    def stage_copies(ex, s, sl):
        span = pl.ds(s * W_STAGE, W_STAGE)
        full = pl.ds(0, D_MODEL)
        return (pltpu.make_async_copy(wg_hbm.at[ex, full, span], sg_sc.at[sl], wsem.at[sl, 0]),
                pltpu.make_async_copy(wu_hbm.at[ex, full, span], su_sc.at[sl], wsem.at[sl, 1]),
                pltpu.make_async_copy(wd_hbm.at[ex, span, full], sd_sc.at[sl], wsem.at[sl, 2]))

    def start_stage(ex, s, sl):
        for cp in stage_copies(ex, s, sl):
            cp.start(priority=1)

    def prefetch_next():
        for s in range(N_WSLOTS):
            start_stage(e_next, s, s)

    def ffn_chunk(x, acc, lo, width):
        cols = slice(lo, lo + width)
        g = _dot(x, wg_sc[:, cols]) + bg_ref[0, :, cols]
        u = _dot(x, wu_sc[:, cols]) + bu_ref[0, :, cols]
        g = jnp.minimum(g, SWIGLU_LIMIT)
        u = jnp.clip(u, -SWIGLU_LIMIT, SWIGLU_LIMIT)
        act = (u + 1.0) * (g * jax.nn.sigmoid(SWIGLU_ALPHA * g))
        return acc + _dot(act.astype(BF16), wd_sc[cols, :])

    @pl.when(jnp.logical_and(c == 0, n_act > 0))
    def _():
        gather(tokc_ref, 0)
        for s in range(N_WSLOTS):
            start_stage(e, s, s)

    @pl.when(c + 1 < n_act)
    def _():
        gather(tokn_ref, 1 - slot)

    @pl.when(c < n_act)
    def _():
        _drain_rows(MOE_ROWS, _row_copy(h_hbm, xbuf.at[slot], 0, 0, sem.at[slot]))
        x = xbuf[slot].astype(BF16)

        @pl.when(is_first)
        def _():
            acc = jnp.zeros((MOE_ROWS, D_MODEL), F32)
            for s in range(D_FF // W_STAGE):
                sl = s % N_WSLOTS
                span = slice(s * W_STAGE, (s + 1) * W_STAGE)
                for cp in stage_copies(e, s, sl):
                    cp.wait()
                wg_sc[:, span] = sg_sc[sl].astype(BF16)
                wu_sc[:, span] = su_sc[sl].astype(BF16)
                wd_sc[span, :] = sd_sc[sl].astype(BF16)
                if s + N_WSLOTS < D_FF // W_STAGE:
                    start_stage(e, s + N_WSLOTS, sl)
                acc = ffn_chunk(x, acc, s * W_STAGE, W_STAGE)
            y_ref[...] = acc + bd_ref[0]

            @pl.when(next_switch)
            def _():
                prefetch_next()

        @pl.when(jnp.logical_not(is_first))
        def _():
            @pl.when(next_switch)
            def _():
                prefetch_next()

            acc = jnp.zeros((MOE_ROWS, D_MODEL), F32)
            for f in range(D_FF // FF_CHUNK):
                acc = ffn_chunk(x, acc, f * FF_CHUNK, FF_CHUNK)
            y_ref[...] = acc + bd_ref[0]

    @pl.when(c >= n_act)
    def _():
        y_ref[...] = jnp.zeros(y_ref.shape, y_ref.dtype)


def _experts(chunk_e, n_active, row_tok, h2, wg, bg, wu, bu, wd, bd, n_chunks):
    def wspec():
        return pl.BlockSpec(memory_space=pl.ANY)

    def bspec():
        return pl.BlockSpec((1, 1, D_FF), lambda c, ce, na: (ce[c], 0, 0))

    tok3 = row_tok.reshape(n_chunks, 1, MOE_ROWS)
    grid_spec = pltpu.PrefetchScalarGridSpec(
        num_scalar_prefetch=2,
        grid=(n_chunks,),
        in_specs=[
            pl.BlockSpec((1, 1, MOE_ROWS), lambda c, ce, na: (c, 0, 0), memory_space=pltpu.SMEM),
            pl.BlockSpec((1, 1, MOE_ROWS), lambda c, ce, na: (jnp.minimum(c + 1, n_chunks - 1), 0, 0),
                         memory_space=pltpu.SMEM),
            pl.BlockSpec(memory_space=pl.ANY),
            wspec(), bspec(),
            wspec(), bspec(),
            wspec(), bspec(),
        ],
        out_specs=pl.BlockSpec((MOE_ROWS, D_MODEL), lambda c, ce, na: (c, 0)),
        scratch_shapes=[
            pltpu.VMEM((2, MOE_ROWS, D_MODEL), F32),
            pltpu.SemaphoreType.DMA((2,)),
            pltpu.VMEM((D_MODEL, D_FF), BF16),
            pltpu.VMEM((D_MODEL, D_FF), BF16),
            pltpu.VMEM((D_FF, D_MODEL), BF16),
            pltpu.VMEM((N_WSLOTS, D_MODEL, W_STAGE), F32),
            pltpu.VMEM((N_WSLOTS, D_MODEL, W_STAGE), F32),
            pltpu.VMEM((N_WSLOTS, W_STAGE, D_MODEL), F32),
            pltpu.SemaphoreType.DMA((N_WSLOTS, 3)),
        ],
    )
    return pl.pallas_call(
        _experts_kernel,
        grid_spec=grid_spec,
        out_shape=jax.ShapeDtypeStruct((n_chunks * MOE_ROWS, D_MODEL), F32),
        compiler_params=_params(("arbitrary",), VMEM_LIMIT_EXPERTS),
        name="experts",
    )(chunk_e, n_active, tok3, tok3, h2, wg, bg, wu, bu, wd, bd)


def _combine_kernel(destc_ref, destn_ref, gate_ref, x1_ref, gf_ref, y_hbm, o_ref, ybuf, sem):
    i = pl.program_id(0)
    slot = i & 1
    n = TB_COMBINE * TOP_K

    def gather(dest_ref, s_):
        _issue_rows(n, lambda r: _row_copy(y_hbm, ybuf.at[s_, r & (TOP_K - 1)], dest_ref[0, 0, r],
                                           r >> TOP_K_LOG2, sem.at[s_]), n_queues=2)

    @pl.when(i == 0)
    def _():
        gather(destc_ref, 0)

    @pl.when(i + 1 < pl.num_programs(0))
    def _():
        gather(destn_ref, 1 - slot)

    _drain_rows(n, _row_copy(y_hbm, ybuf.at[slot, 0], 0, 0, sem.at[slot]))
    gate = gate_ref[...]
    out = x1_ref[...]
    for k in range(TOP_K):
        out = out + gate[:, k:k + 1] * ybuf[slot, k]
    o_ref[...] = out * lax.rsqrt(jnp.mean(out * out, axis=-1, keepdims=True) + NORM_EPS) * gf_ref[...]


def _combine(dest, gate, x1, gf, y_rows):
    t = x1.shape[0]
    nt = t // TB_COMBINE
    dest3 = dest.reshape(nt, 1, TB_COMBINE * TOP_K)
    return pl.pallas_call(
        _combine_kernel,
        grid=(nt,),
        in_specs=[
            pl.BlockSpec((1, 1, TB_COMBINE * TOP_K), lambda i: (i, 0, 0), memory_space=pltpu.SMEM),
            pl.BlockSpec((1, 1, TB_COMBINE * TOP_K), lambda i: (jnp.minimum(i + 1, nt - 1), 0, 0),
                         memory_space=pltpu.SMEM),
            pl.BlockSpec((TB_COMBINE, LANES), lambda i: (i, 0)),
            pl.BlockSpec((TB_COMBINE, D_MODEL), lambda i: (i, 0)),
            pl.BlockSpec((1, D_MODEL), lambda i: (0, 0)),
            pl.BlockSpec(memory_space=pl.ANY),
        ],
        out_specs=pl.BlockSpec((TB_COMBINE, D_MODEL), lambda i: (i, 0)),
        out_shape=jax.ShapeDtypeStruct((t, D_MODEL), F32),
        scratch_shapes=[
            pltpu.VMEM((2, TOP_K, TB_COMBINE, D_MODEL), F32),
            pltpu.SemaphoreType.DMA((2,)),
        ],
        compiler_params=_params(("arbitrary",)),
        name="combine",
    )(dest3, dest3, gate, x1, gf, y_rows)


def _rope_tables(s):
    pos = jnp.arange(s, dtype=F32)
    inv = ROPE_THETA ** (-jnp.arange(0, HEAD_DIM, 2, dtype=F32) / HEAD_DIM)
    ang = pos[:, None] * inv[None, :]
    ang = jnp.concatenate([ang, ang], axis=-1)
    sign = jnp.where(jnp.arange(HEAD_DIM) < HEAD_DIM // 2, -1.0, 1.0).astype(F32)
    return jnp.cos(ang), jnp.sin(ang) * sign[None, :]


def _cmp_to_sel_t(nc, n_sel):
    cs = np.arange(nc) * CMP_STRIDE
    ss = np.arange(n_sel) * SEL_BLOCK
    ov = np.minimum(cs[:, None] + CMP_BLOCK, ss[None, :] + SEL_BLOCK) - np.maximum(cs[:, None], ss[None, :])
    m = np.clip(ov, 0, None) / CMP_BLOCK
    m[nc - 1, :] = 0.0
    out = np.zeros((LANES, nc), np.float32)
    out[:n_sel] = m.T
    return jnp.asarray(out, dtype=BF16)


def _block_expand(s):
    key = np.arange(s).reshape(s // TK_SEL, TK_SEL, 1)
    blk = np.arange(LANES).reshape(1, 1, LANES)
    return jnp.asarray((key // SEL_BLOCK == blk).astype(np.float32), dtype=BF16)


def kernel(x, norm1_g, w_in, cmp_pe_k, cmp_pe_v, cmp_k_w1, cmp_k_w2, cmp_v_w1, cmp_v_w2, lru_conv_w, lru_conv_b, lru_wa, lru_ba, lru_wi, lru_bi, lru_lambda, w_br_attn, w_br_lru, w_out, norm2_g, router_w, router_b, exp_w_gate, exp_b_gate, exp_w_up, exp_b_up, exp_w_down, exp_b_down, normf_g):
    b, s, d = x.shape
    assert d == D_MODEL and norm1_g.shape[0] == 1, "single-layer block of width D_MODEL"
    assert s % TM == 0 and s % TS_LRU == 0 and s >= WIN_KEYS and s // SEL_BLOCK <= LANES
    assert s % (2 * TK_SEL) == 0, "the selected branch walks key tiles in pairs"
    t = b * s
    x2 = x.reshape(t, d)

    w_in0 = w_in[0]
    o = 0
    wq = w_in0[:, o:o + Q_W].astype(BF16); o += Q_W
    wkv = w_in0[:, o:o + KV_KINDS * KV_W].reshape(d, KV_KINDS, KV_W); o += KV_KINDS * KV_W
    wk4 = wkv[:, (0, 1, 2, 4), :].reshape(d, ROW_KINDS * KV_W).astype(BF16)
    wv2 = wkv[:, (3, 5), :].reshape(d, 2 * KV_W).astype(BF16)
    n_g = 3 * NSA_Q_HEADS
    wg_nsa = w_in0[:, o:o + n_g].reshape(d, NSA_KV_GROUPS, 3 * NSA_GROUP); o += n_g
    wg_nsa = jnp.pad(wg_nsa, ((0, 0), (0, 0), (0, LANES - 3 * NSA_GROUP))).reshape(d, NSA_KV_GROUPS * LANES).astype(BF16)
    wlxy = w_in0[:, o:o + 2 * LRU_WIDTH].astype(BF16); o += 2 * LRU_WIDTH
    wga = w_in0[:, o:o + D_MODEL].astype(BF16); o += D_MODEL
    wgb = w_in0[:, o:o + D_MODEL].astype(BF16); o += D_MODEL

    cos, sin_signed = _rope_tables(s)
    nc = s // CMP_STRIDE
    n_sel = s // SEL_BLOCK

    h, gates = _norm1(x2, norm1_g, wg_nsa)
    q_raw, q_rot = _qproj(h, wq, cos, sin_signed, s)
    kv4 = _kvproj(h, wk4, cos, sin_signed, b, s)
    vt2 = _vtproj(h, wv2, b, s)
    lxy = _proj(h, wlxy, "lruproj")

    cw1 = jnp.stack([cmp_k_w1[0], cmp_v_w1[0]]).astype(BF16)
    cw2 = jnp.stack([cmp_k_w2[0], cmp_v_w2[0]]).astype(BF16)
    pe = jnp.stack([cmp_pe_k[0], cmp_pe_v[0]]).reshape(2, 1, CMP_BLOCK * HEAD_DIM)
    pe = jnp.pad(pe, ((0, 0), (0, SUBLANES - 1), (0, 0))).astype(BF16)
    kvc = _cmpkv(kv4, cw1, cw2, pe, b, s)
    ocmp, msk = _cmpattn(q_raw, kvc, _cmp_to_sel_t(nc, n_sel), b, s)
    o_a = _selwin(q_rot, kv4, vt2, msk, _block_expand(s), ocmp, gates, b, s)

    o_b = _lru(lxy, lru_conv_w[0], lru_conv_b, lru_wa[0].astype(BF16), lru_ba[0].reshape(1, LRU_WIDTH),
               lru_wi[0].astype(BF16), lru_bi[0].reshape(1, LRU_WIDTH), lru_lambda, b, s)

    merged = _merge(h, o_a.reshape(t, Q_W), o_b.reshape(t, LRU_WIDTH), wga, wgb,
                    w_br_attn[0].astype(BF16), w_br_lru[0].astype(BF16))

    rw = jnp.pad(router_w[0], ((0, 0), (0, LANES - N_EXPERTS))).astype(BF16)
    rb = jnp.pad(router_b, ((0, 0), (0, LANES - N_EXPERTS)))
    tri = jnp.asarray(np.tril(np.ones((TM_OUT, TM_OUT), np.float32), -1), dtype=BF16)
    x1, h2, tope, gate, rank, cnt = _outrouter(merged, x2, w_out[0].astype(BF16), norm2_g, rw, rb, tri)

    counts = cnt[0, :N_EXPERTS]
    padded = (counts + MOE_ROWS - 1) // MOE_ROWS * MOE_ROWS
    pend = jnp.cumsum(padded)
    pstart = pend - padded
    n_rows = t * TOP_K + N_EXPERTS * MOE_ROWS
    n_chunks = n_rows // MOE_ROWS
    dest = (pstart[tope[:, :TOP_K]] + rank[:, :TOP_K]).astype(I32)
    tok_of = jnp.broadcast_to(jnp.arange(t, dtype=I32)[:, None], (t, TOP_K))
    row_tok = jnp.zeros((n_rows,), I32).at[dest.reshape(-1)].set(tok_of.reshape(-1))
    chunk_e = jnp.minimum(jnp.sum((jnp.arange(n_chunks) * MOE_ROWS)[:, None] >= pend[None, :], axis=1),
                          N_EXPERTS - 1).astype(I32)
    n_active = (pend[-1:] // MOE_ROWS).astype(I32)

    y_rows = _experts(chunk_e, n_active, row_tok, h2,
                      exp_w_gate.reshape(N_EXPERTS, D_MODEL, D_FF), exp_b_gate[0].reshape(N_EXPERTS, 1, D_FF),
                      exp_w_up.reshape(N_EXPERTS, D_MODEL, D_FF), exp_b_up[0].reshape(N_EXPERTS, 1, D_FF),
                      exp_w_down.reshape(N_EXPERTS, D_FF, D_MODEL), exp_b_down[0].reshape(N_EXPERTS, 1, D_MODEL),
                      n_chunks)
    out = _combine(dest, gate, x1, normf_g.reshape(1, D_MODEL), y_rows)
    return out.reshape(b, s, d)
```

```python
import functools

import numpy as np
import jax
import jax.numpy as jnp
from jax import lax
from jax.experimental import pallas as pl
from jax.experimental.pallas import tpu as pltpu

F32 = jnp.float32
BF16 = jnp.bfloat16
I32 = jnp.int32

D_MODEL = 2048
HEAD_DIM = 128
NSA_Q_HEADS = 16
NSA_KV_GROUPS = 4
NSA_GROUP = NSA_Q_HEADS // NSA_KV_GROUPS
CMP_BLOCK = 32
CMP_STRIDE = 16
CMP_HIDDEN = 256
SEL_BLOCK = 64
SEL_BLOCK_LOG2 = 6
SEL_TOPK = 16
WINDOW = 512
ROPE_THETA = 10000.0
LRU_WIDTH = 2048
LRU_BLOCKS = 16
LRU_BLOCK_W = LRU_WIDTH // LRU_BLOCKS
CONV_WIDTH = 4
LRU_C = 8.0
N_EXPERTS = 32
TOP_K = 4
TOP_K_LOG2 = 2
D_FF = 2048
SWIGLU_ALPHA = 1.702
SWIGLU_LIMIT = 7.0
NORM_EPS = 1e-6
NEG_INF = -1e30
FORCE_SCORE = 1e6

Q_W = NSA_Q_HEADS * HEAD_DIM
KV_W = NSA_KV_GROUPS * HEAD_DIM
GROUP_W = NSA_GROUP * HEAD_DIM
SCALE = HEAD_DIM ** -0.5
LOG2_E = 1.4426950408889634
LANES = 128
SUBLANES = 8

TM = 1024
TN = 512
TN_WIDE = 1024
TQ_CMP = 256
TQ = 256
TQ_LOG2 = 8
TK_SEL = 256
WIN_KEYS = WINDOW + TQ
VT_ROWS = HEAD_DIM + 16
TS_LRU = 256
CW_LRU = 512
TM_OUT = 256
MOE_ROWS = 256
FF_CHUNK = 512
TB_COMBINE = 128
DMA_UNROLL = 8
W_STAGE = 256
N_WSLOTS = 2
VMEM_LIMIT = 56 * 1024 * 1024
VMEM_LIMIT_EXPERTS = 60 * 1024 * 1024


def _params(sem, vmem=None):
    return pltpu.CompilerParams(dimension_semantics=sem, vmem_limit_bytes=vmem)


def _dot(a, b):
    return jnp.dot(a, b, preferred_element_type=F32)


def _dot_nt(a, b, precision=None):
    return lax.dot_general(a, b, (((1,), (1,)), ((), ())), precision=precision, preferred_element_type=F32)


def _norm1_kernel(x_ref, g_ref, wg_ref, h_ref, gate_ref):
    x = x_ref[...]
    y = x * lax.rsqrt(jnp.mean(x * x, axis=-1, keepdims=True) + NORM_EPS) * g_ref[...]
    hb = y.astype(BF16)
    h_ref[...] = hb
    gate_ref[...] = jax.nn.sigmoid(_dot(hb, wg_ref[...]))


def _norm1(x2, g, wg):
    t = x2.shape[0]
    return pl.pallas_call(
        _norm1_kernel,
        grid=(t // TM,),
        in_specs=[
            pl.BlockSpec((TM, D_MODEL), lambda i: (i, 0)),
            pl.BlockSpec((1, D_MODEL), lambda i: (0, 0)),
            pl.BlockSpec((D_MODEL, NSA_KV_GROUPS * LANES), lambda i: (0, 0)),
        ],
        out_specs=[
            pl.BlockSpec((TM, D_MODEL), lambda i: (i, 0)),
            pl.BlockSpec((TM, NSA_KV_GROUPS * LANES), lambda i: (i, 0)),
        ],
        out_shape=[
            jax.ShapeDtypeStruct((t, D_MODEL), BF16),
            jax.ShapeDtypeStruct((t, NSA_KV_GROUPS * LANES), F32),
        ],
        compiler_params=_params(("parallel",)),
        name="norm1",
    )(x2, g, wg)


def _rope(x, cos, sin_signed):
    return x * cos + pltpu.roll(x, HEAD_DIM // 2, axis=1) * sin_signed


def _qproj_kernel(h_ref, w_ref, cos_ref, sin_ref, qraw_ref, qrot_ref):
    acc = _dot(h_ref[...], w_ref[...])
    qraw_ref[...] = acc.astype(BF16)
    cos = cos_ref[...]
    sin = sin_ref[...]
    for hh in range(TN_WIDE // HEAD_DIM):
        sl = slice(hh * HEAD_DIM, (hh + 1) * HEAD_DIM)
        qrot_ref[:, sl] = _rope(acc[:, sl], cos, sin).astype(BF16)


def _qproj(h, wq, cos, sin, s):
    t = h.shape[0]
    ns = s // TM
    return pl.pallas_call(
        _qproj_kernel,
        grid=(Q_W // TN_WIDE, t // TM),
        in_specs=[
            pl.BlockSpec((TM, D_MODEL), lambda j, i: (i, 0)),
            pl.BlockSpec((D_MODEL, TN_WIDE), lambda j, i: (0, j)),
            pl.BlockSpec((TM, HEAD_DIM), lambda j, i: (i % ns, 0)),
            pl.BlockSpec((TM, HEAD_DIM), lambda j, i: (i % ns, 0)),
        ],
        out_specs=[
            pl.BlockSpec((TM, TN_WIDE), lambda j, i: (i, j)),
            pl.BlockSpec((TM, TN_WIDE), lambda j, i: (i, j)),
        ],
        out_shape=[jax.ShapeDtypeStruct((t, Q_W), BF16), jax.ShapeDtypeStruct((t, Q_W), BF16)],
        compiler_params=_params(("parallel", "parallel")),
        name="qproj",
    )(h, wq, cos, sin)


KV_KINDS = 6
ROW_KINDS = 4
ROW_K_SEL = 2
ROW_K_WIN = 3


def _kvproj_kernel(h_ref, w_ref, cos_ref, sin_ref, out_ref):
    kind = pl.program_id(0)
    acc = _dot(h_ref[...], w_ref[...])
    cos = cos_ref[...]
    sin = sin_ref[...]
    is_rope = jnp.logical_or(kind == ROW_K_SEL, kind == ROW_K_WIN)
    for g in range(NSA_KV_GROUPS):
        xh = acc[:, g * HEAD_DIM:(g + 1) * HEAD_DIM]
        out_ref[0, 0, g] = jnp.where(is_rope, _rope(xh, cos, sin), xh).astype(BF16)


def _kvproj(h, wk4, cos, sin, b, s):
    t = h.shape[0]
    ns = s // TM
    return pl.pallas_call(
        _kvproj_kernel,
        grid=(ROW_KINDS, t // TM),
        in_specs=[
            pl.BlockSpec((TM, D_MODEL), lambda k, i: (i, 0)),
            pl.BlockSpec((D_MODEL, KV_W), lambda k, i: (0, k)),
            pl.BlockSpec((TM, HEAD_DIM), lambda k, i: (i % ns, 0)),
            pl.BlockSpec((TM, HEAD_DIM), lambda k, i: (i % ns, 0)),
        ],
        out_specs=pl.BlockSpec((1, 1, NSA_KV_GROUPS, TM, HEAD_DIM), lambda k, i: (k, i // ns, 0, i % ns, 0)),
        out_shape=jax.ShapeDtypeStruct((ROW_KINDS, b, NSA_KV_GROUPS, s, HEAD_DIM), BF16),
        compiler_params=_params(("parallel", "parallel")),
        name="kvproj",
    )(h, wk4, cos, sin)


def _vtproj_kernel(h_ref, w_ref, out_ref):
    acc = _dot(h_ref[...], w_ref[...])
    sub = lax.broadcasted_iota(I32, (VT_ROWS - HEAD_DIM, TK_SEL), 0)
    ones_row = jnp.where(sub == 0, 1.0, 0.0).astype(BF16)
    for g in range(NSA_KV_GROUPS):
        for tt in range(TM // TK_SEL):
            tile = acc[tt * TK_SEL:(tt + 1) * TK_SEL, g * HEAD_DIM:(g + 1) * HEAD_DIM]
            out_ref[0, 0, g, tt, :HEAD_DIM, :] = tile.T.astype(BF16)
            out_ref[0, 0, g, tt, HEAD_DIM:, :] = ones_row


def _vtproj(h, wv2, b, s):
    t = h.shape[0]
    ns = s // TM
    nt = TM // TK_SEL
    return pl.pallas_call(
        _vtproj_kernel,
        grid=(2, t // TM),
        in_specs=[
            pl.BlockSpec((TM, D_MODEL), lambda k, i: (i, 0)),
            pl.BlockSpec((D_MODEL, KV_W), lambda k, i: (0, k)),
        ],
        out_specs=pl.BlockSpec((1, 1, NSA_KV_GROUPS, nt, VT_ROWS, TK_SEL),
                               lambda k, i: (k, i // ns, 0, i % ns, 0, 0)),
        out_shape=jax.ShapeDtypeStruct((2, b, NSA_KV_GROUPS, s // TK_SEL, VT_ROWS, TK_SEL), BF16),
        compiler_params=_params(("parallel", "parallel")),
        name="vtproj",
    )(h, wv2)


def _proj_kernel(h_ref, w_ref, o_ref):
    o_ref[...] = _dot(h_ref[...], w_ref[...]).astype(o_ref.dtype)


def _proj(h, w, name):
    t = h.shape[0]
    n = w.shape[1]
    return pl.pallas_call(
        _proj_kernel,
        grid=(n // TN_WIDE, t // TM),
        in_specs=[
            pl.BlockSpec((TM, D_MODEL), lambda j, i: (i, 0)),
            pl.BlockSpec((D_MODEL, TN_WIDE), lambda j, i: (0, j)),
        ],
        out_specs=pl.BlockSpec((TM, TN_WIDE), lambda j, i: (i, j)),
        out_shape=jax.ShapeDtypeStruct((t, n), BF16),
        compiler_params=_params(("parallel", "parallel")),
        name=name,
    )(h, w)


def _cmpkv_kernel(c_ref, w1_ref, w2_ref, pe_ref, o_ref):
    c = c_ref[0, 0, 0]
    half = CMP_STRIDE * HEAD_DIM
    top = _dot(c, w1_ref[0, :half, :])
    bot = _dot(c, w1_ref[0, half:, :])
    n_rows = top.shape[0]
    bot_next = pltpu.roll(bot, n_rows - 1, axis=0)
    bias = _dot(pe_ref[0], w1_ref[0])[0:1, :]
    hid = jax.nn.gelu(top + bot_next + bias)
    o_ref[0, 0, 0] = _dot(hid.astype(BF16), w2_ref[0]).astype(BF16)


def _cmpkv(kv4, w1, w2, pe, b, s):
    nc = s // CMP_STRIDE
    kvc = kv4[:2].reshape(2, b, NSA_KV_GROUPS, nc, CMP_STRIDE * HEAD_DIM)
    return pl.pallas_call(
        _cmpkv_kernel,
        grid=(2, b, NSA_KV_GROUPS),
        in_specs=[
            pl.BlockSpec((1, 1, 1, nc, CMP_STRIDE * HEAD_DIM), lambda k, bi, g: (k, bi, g, 0, 0)),
            pl.BlockSpec((1, CMP_BLOCK * HEAD_DIM, CMP_HIDDEN), lambda k, bi, g: (k, 0, 0)),
            pl.BlockSpec((1, CMP_HIDDEN, HEAD_DIM), lambda k, bi, g: (k, 0, 0)),
            pl.BlockSpec((1, SUBLANES, CMP_BLOCK * HEAD_DIM), lambda k, bi, g: (k, 0, 0)),
        ],
        out_specs=pl.BlockSpec((1, 1, 1, nc, HEAD_DIM), lambda k, bi, g: (k, bi, g, 0, 0)),
        out_shape=jax.ShapeDtypeStruct((2, b, NSA_KV_GROUPS, nc, HEAD_DIM), BF16),
        compiler_params=_params(("parallel", "parallel", "parallel")),
        name="cmpkv",
    )(kvc, w1, w2, pe)


def _cmpattn_kernel(q_ref, kc_ref, vc_ref, mt_ref, ocmp_ref, msk_ref, *, n_sel):
    i = pl.program_id(2)
    tq = TQ_CMP
    q = q_ref[0]
    kc = kc_ref[0, 0, 0]
    vc = vc_ref[0, 0, 0]
    nc = kc.shape[0]
    t_row = i * tq + lax.broadcasted_iota(I32, (tq, 1), 0)
    n_lane = lax.broadcasted_iota(I32, (tq, nc), 1)
    cmask = (n_lane * CMP_STRIDE + (CMP_BLOCK - 1)) <= t_row
    psum = jnp.zeros((tq, nc), F32)
    for hh in range(NSA_GROUP):
        sl = slice(hh * HEAD_DIM, (hh + 1) * HEAD_DIM)
        s = jnp.where(cmask, _dot_nt(q[:, sl], kc), NEG_INF)
        e = jnp.exp2((s - jnp.max(s, axis=-1, keepdims=True)) * (SCALE * LOG2_E))
        p = jnp.where(cmask, e * (1.0 / jnp.sum(e, axis=-1, keepdims=True)), 0.0)
        ocmp_ref[0, :, sl] = _dot(p.astype(BF16), vc).astype(BF16)
        psum = psum + p
    imp_t = _dot_nt(mt_ref[...], psum.astype(BF16))
    imp_t = imp_t[:n_sel]
    jb = lax.broadcasted_iota(I32, (n_sel, tq), 0)
    cur = (i * tq + lax.broadcasted_iota(I32, (n_sel, tq), 1)) >> SEL_BLOCK_LOG2
    forced = (jb == 0) | (jb == cur) | (jb == cur - 1)
    valid = jb <= cur
    score = jnp.where(forced, FORCE_SCORE, jnp.where(valid, imp_t, NEG_INF))
    pad = jnp.zeros((LANES - n_sel, tq), F32)
    needs_rank = (i + 1) * tq > SEL_TOPK * SEL_BLOCK

    @pl.when(needs_rank)
    def _():
        n_grp = n_sel // SUBLANES
        grp = [score[gi * SUBLANES:(gi + 1) * SUBLANES] for gi in range(n_grp)]
        jb_grp = lax.broadcasted_iota(I32, (SUBLANES, tq), 0)
        ranks = [jnp.zeros((SUBLANES, tq), F32) for _ in range(n_grp)]
        for jp in range(n_sel):
            row = score[jp:jp + 1, :]
            for gi in range(n_grp):
                lo = gi * SUBLANES
                if lo > jp:
                    beats = row >= grp[gi]
                elif lo + SUBLANES - 1 <= jp:
                    beats = row > grp[gi]
                else:
                    beats = (row > grp[gi]) | ((row == grp[gi]) & (jb_grp > jp - lo))
                ranks[gi] = ranks[gi] + jnp.where(beats, 1.0, 0.0)
        rank = jnp.concatenate(ranks, axis=0)
        member = (valid & (rank < float(SEL_TOPK))).astype(F32)
        msk_ref[0, 0] = jnp.concatenate([member, pad], axis=0).astype(BF16)

    @pl.when(jnp.logical_not(needs_rank))
    def _():
        msk_ref[0, 0] = jnp.concatenate([valid.astype(F32), pad], axis=0).astype(BF16)


def _cmpattn(q_raw, kvc, mt, b, s):
    nc = s // CMP_STRIDE
    n_sel = s // SEL_BLOCK
    q3 = q_raw.reshape(b, s, Q_W)
    return pl.pallas_call(
        functools.partial(_cmpattn_kernel, n_sel=n_sel),
        grid=(b, NSA_KV_GROUPS, s // TQ_CMP),
        in_specs=[
            pl.BlockSpec((1, TQ_CMP, GROUP_W), lambda bi, g, i: (bi, i, g)),
            pl.BlockSpec((1, 1, 1, nc, HEAD_DIM), lambda bi, g, i: (0, bi, g, 0, 0)),
            pl.BlockSpec((1, 1, 1, nc, HEAD_DIM), lambda bi, g, i: (1, bi, g, 0, 0)),
            pl.BlockSpec((LANES, nc), lambda bi, g, i: (0, 0)),
        ],
        out_specs=[
            pl.BlockSpec((1, TQ_CMP, GROUP_W), lambda bi, g, i: (bi, i, g)),
            pl.BlockSpec((1, 1, LANES, TQ_CMP), lambda bi, g, i: (bi, g, 0, i)),
        ],
        out_shape=[
            jax.ShapeDtypeStruct((b, s, Q_W), BF16),
            jax.ShapeDtypeStruct((b, NSA_KV_GROUPS, LANES, s), BF16),
        ],
        compiler_params=_params(("parallel", "parallel", "parallel")),
        name="cmpattn",
    )(q3, kvc, kvc, mt)


def _selwin_kernel(q_ref, ks_ref, vs_ref, kw_ref, vw_ref, msk_ref, emat_ref, ocmp_ref, gate_ref, o_ref,
                   acc_sc, accw_sc, sa_sc, sb_sc, ba_sc, bb_sc):
    assert TQ == TK_SEL, "the window walk assumes key tile n covers the queries of grid step n"
    i = pl.program_id(2)
    tq = TQ
    q = q_ref[0]
    heads = [slice(hh * tq, (hh + 1) * tq) for hh in range(NSA_GROUP)]
    q_t = jnp.concatenate(
        [q[:, hh * HEAD_DIM:(hh + 1) * HEAD_DIM].astype(F32).T.astype(BF16) for hh in range(NSA_GROUP)], axis=1)
    t_lane = i * tq + lax.broadcasted_iota(I32, (1, tq), 1)

    c_exp = SCALE * LOG2_E

    def make_branch(lhs_of, rhs, v_ref, acc_ref, visible):
        def scores(j, s_ref, b_ref=None):
            off = pl.multiple_of(j * TK_SEL, TK_SEL)
            s_ref[...] = _dot(lhs_of(j, off), rhs)
            if b_ref is not None:
                kpos = off + lax.broadcasted_iota(I32, (TK_SEL, tq), 0)
                b_ref[...] = jnp.where(visible(kpos), 0.0, NEG_INF)

        def absorb(j, s_ref, m_old, b_ref=None):
            v_t = v_ref[0, 0, 0, j]
            m_new = []
            for hh in range(NSA_GROUP):
                sh = s_ref[:, heads[hh]]
                if b_ref is not None:
                    sh = sh + b_ref[...]
                mh = jnp.maximum(m_old[hh], jnp.max(sh, axis=0, keepdims=True))
                alpha = jnp.exp2((m_old[hh] - mh) * c_exp)
                p = jnp.exp2((sh - mh) * c_exp).astype(BF16)
                acc_ref[:, heads[hh]] = acc_ref[:, heads[hh]] * alpha + _dot(v_t, p)
                m_new.append(mh)
            return tuple(m_new)

        return scores, absorb

    m0 = tuple(jnp.full((1, tq), NEG_INF, F32) for _ in range(NSA_GROUP))

    not_picked = ((msk_ref[0, 0].astype(F32) - 1.0) * (-NEG_INF)).astype(BF16)
    rhs_sel = jnp.concatenate([q_t, jnp.concatenate([not_picked] * NSA_GROUP, axis=1)], axis=0)

    def sel_lhs(j, off):
        return jnp.concatenate([ks_ref[0, 0, 0, pl.ds(off, TK_SEL), :], emat_ref[j]], axis=1)

    def causal(kpos):
        return kpos <= t_lane

    sel_scores, sel_absorb = make_branch(sel_lhs, rhs_sel, vs_ref, acc_sc, causal)
    acc_sc[...] = jnp.zeros(acc_sc.shape, F32)

    n_full = i >> 1
    sel_scores(0, sa_sc)

    def pair(jj, carry):
        j0 = 2 * jj
        sel_scores(j0 + 1, sb_sc)
        carry = sel_absorb(j0, sa_sc, carry)
        sel_scores(j0 + 2, sa_sc)
        return sel_absorb(j0 + 1, sb_sc, carry)

    m_sel = lax.fori_loop(0, n_full, pair, m0)
    kpos_a = 2 * n_full * TK_SEL + lax.broadcasted_iota(I32, (TK_SEL, tq), 0)
    ba_sc[...] = jnp.where(causal(kpos_a), 0.0, NEG_INF)
    sel_scores(2 * n_full + 1, sb_sc, bb_sc)
    m_sel = sel_absorb(2 * n_full, sa_sc, m_sel, ba_sc)
    sel_absorb(2 * n_full + 1, sb_sc, m_sel, bb_sc)

    def win_visible(kpos):
        return (kpos <= t_lane) & (kpos > t_lane - WINDOW)

    def win_lhs(j, off):
        return kw_ref[0, 0, 0, pl.ds(off, TK_SEL), :]

    win_scores, win_absorb = make_branch(win_lhs, q_t, vw_ref, accw_sc, win_visible)
    accw_sc[...] = jnp.zeros(accw_sc.shape, F32)
    n_back = WINDOW // TK_SEL
    tiles = [i] + [jnp.where(i >= kb, i - kb, kb) for kb in range(1, n_back + 1)]
    bufs = [(sa_sc, ba_sc), (sb_sc, bb_sc)]
    win_scores(tiles[0], *bufs[0])
    m_win = m0
    for n, tile in enumerate(tiles):
        if n + 1 < len(tiles):
            win_scores(tiles[n + 1], *bufs[(n + 1) % 2])
        s_buf, b_buf = bufs[n % 2]
        m_win = win_absorb(tile, s_buf, m_win, b_buf)

    gate = gate_ref[0]
    gate_t = gate.T
    ocmp = ocmp_ref[0]
    for hh in range(NSA_GROUP):
        cols = slice(hh * HEAD_DIM, (hh + 1) * HEAD_DIM)
        o_sel_t = acc_sc[:HEAD_DIM, heads[hh]] * (1.0 / acc_sc[HEAD_DIM:HEAD_DIM + 1, heads[hh]])
        o_win_t = accw_sc[:HEAD_DIM, heads[hh]] * (1.0 / accw_sc[HEAD_DIM:HEAD_DIM + 1, heads[hh]])
        z_t = gate_t[3 * hh + 1:3 * hh + 2, :] * o_sel_t + gate_t[3 * hh + 2:3 * hh + 3, :] * o_win_t
        o = gate[:, 3 * hh:3 * hh + 1] * ocmp[:, cols].astype(F32) + z_t.T
        o_ref[0, :, cols] = o.astype(BF16)


def _selwin(q_rot, kv4, vt2, msk, emat, ocmp, gates, b, s):
    q3 = q_rot.reshape(b, s, Q_W)
    g3 = gates.reshape(b, s, NSA_KV_GROUPS * LANES)

    def k_spec(kind):
        return pl.BlockSpec((1, 1, 1, s, HEAD_DIM), lambda bi, g, i: (kind, bi, g, 0, 0))

    def vt_spec(kind):
        return pl.BlockSpec((1, 1, 1, s // TK_SEL, VT_ROWS, TK_SEL), lambda bi, g, i: (kind, bi, g, 0, 0, 0))

    return pl.pallas_call(
        _selwin_kernel,
        grid=(b, NSA_KV_GROUPS, s // TQ),
        in_specs=[
            pl.BlockSpec((1, TQ, GROUP_W), lambda bi, g, i: (bi, i, g)),
            k_spec(ROW_K_SEL), vt_spec(0), k_spec(ROW_K_WIN), vt_spec(1),
            pl.BlockSpec((1, 1, LANES, TQ), lambda bi, g, i: (bi, g, 0, i)),
            pl.BlockSpec((s // TK_SEL, TK_SEL, LANES), lambda bi, g, i: (0, 0, 0)),
            pl.BlockSpec((1, TQ, GROUP_W), lambda bi, g, i: (bi, i, g)),
            pl.BlockSpec((1, TQ, LANES), lambda bi, g, i: (bi, i, g)),
        ],
        out_specs=pl.BlockSpec((1, TQ, GROUP_W), lambda bi, g, i: (bi, i, g)),
        out_shape=jax.ShapeDtypeStruct((b, s, Q_W), BF16),
        scratch_shapes=[
            pltpu.VMEM((VT_ROWS, NSA_GROUP * TQ), F32),
            pltpu.VMEM((VT_ROWS, NSA_GROUP * TQ), F32),
            pltpu.VMEM((TK_SEL, NSA_GROUP * TQ), F32),
            pltpu.VMEM((TK_SEL, NSA_GROUP * TQ), F32),
            pltpu.VMEM((TK_SEL, TQ), F32),
            pltpu.VMEM((TK_SEL, TQ), F32),
        ],
        compiler_params=_params(("parallel", "parallel", "parallel")),
        name="selwin",
    )(q3, kv4, vt2, kv4, vt2, msk, emat, ocmp, g3)


def _softplus(x):
    return jnp.maximum(x, 0.0) + jnp.log1p(jnp.exp(-jnp.abs(x)))


def _lru_kernel(lx_ref, ly_ref, cw_ref, cb_ref, wa_ref, ba_ref, wi_ref, bi_ref, lam_ref, o_ref, xe_sc, h_sc):
    ts = TS_LRU
    halo = SUBLANES

    @pl.when(pl.program_id(2) == 0)
    def _():
        xe_sc[0:halo, :] = jnp.zeros((halo, CW_LRU), F32)
        h_sc[...] = jnp.zeros(h_sc.shape, F32)

    x = lx_ref[0].astype(F32)
    xe_sc[halo:halo + ts, :] = x
    xc = cb_ref[...]
    for k in range(CONV_WIDTH):
        d = CONV_WIDTH - 1 - k
        xc = xc + xe_sc[pl.ds(halo - d, ts), :] * cw_ref[k:k + 1, :]
    xe_sc[0:halo, :] = x[ts - halo:ts]

    row_in_group = lax.broadcasted_iota(I32, (ts, LRU_BLOCK_W), 0) & (SUBLANES - 1)
    for n in range(CW_LRU // LRU_BLOCK_W):
        cols = slice(n * LRU_BLOCK_W, (n + 1) * LRU_BLOCK_W)
        xcn = xc[:, cols]
        xb = xcn.astype(BF16)
        r = jax.nn.sigmoid(_dot(xb, wa_ref[n]) + ba_ref[:, cols])
        ig = jax.nn.sigmoid(_dot(xb, wi_ref[n]) + bi_ref[:, cols])
        log_a = (-LRU_C * _softplus(-lam_ref[:, cols])) * r
        a = jnp.exp(log_a)
        u = jnp.sqrt(-jnp.tanh(log_a) * (a * a + 1.0)) * (ig * xcn)
        ca, cb = a, u
        for d in (1, 2, 4):
            ok = row_in_group >= d
            ca_sh = pltpu.roll(ca, d, axis=0)
            cb_sh = pltpu.roll(cb, d, axis=0)
            cb = jnp.where(ok, ca * cb_sh + cb, cb)
            ca = jnp.where(ok, ca * ca_sh, ca)
        hprev = h_sc[:, cols]
        hs = []
        for gi in range(ts // SUBLANES):
            rows = slice(gi * SUBLANES, (gi + 1) * SUBLANES)
            hg = ca[rows] * hprev + cb[rows]
            hs.append(hg)
            hprev = hg[SUBLANES - 1:SUBLANES]
        h_sc[:, cols] = hprev
        h = jnp.concatenate(hs, axis=0)
        o_ref[0, :, cols] = (h * jax.nn.gelu(ly_ref[0, :, cols].astype(F32))).astype(BF16)


def _lru(lxy, conv_w, conv_b, wa, ba, wi, bi, lam, b, s):
    lxy3 = lxy.reshape(b, s, 2 * LRU_WIDTH)
    ncw = LRU_WIDTH // CW_LRU
    nb = CW_LRU // LRU_BLOCK_W
    vec = pl.BlockSpec((1, CW_LRU), lambda bi_, c, t: (0, c))
    return pl.pallas_call(
        _lru_kernel,
        grid=(b, ncw, s // TS_LRU),
        in_specs=[
            pl.BlockSpec((1, TS_LRU, CW_LRU), lambda bi_, c, t: (bi_, t, c)),
            pl.BlockSpec((1, TS_LRU, CW_LRU), lambda bi_, c, t: (bi_, t, ncw + c)),
            pl.BlockSpec((CONV_WIDTH, CW_LRU), lambda bi_, c, t: (0, c)),
            vec,
            pl.BlockSpec((nb, LRU_BLOCK_W, LRU_BLOCK_W), lambda bi_, c, t: (c, 0, 0)),
            vec,
            pl.BlockSpec((nb, LRU_BLOCK_W, LRU_BLOCK_W), lambda bi_, c, t: (c, 0, 0)),
            vec,
            vec,
        ],
        out_specs=pl.BlockSpec((1, TS_LRU, CW_LRU), lambda bi_, c, t: (bi_, t, c)),
        out_shape=jax.ShapeDtypeStruct((b, s, LRU_WIDTH), BF16),
        scratch_shapes=[
            pltpu.VMEM((TS_LRU + SUBLANES, CW_LRU), F32),
            pltpu.VMEM((1, CW_LRU), F32),
        ],
        compiler_params=_params(("parallel", "parallel", "arbitrary")),
        name="lru",
    )(lxy3, lxy3, conv_w, conv_b, wa, ba, wi, bi, lam)


def _merge_kernel(h_ref, oa_ref, ob_ref, wga_ref, wgb_ref, wpa_ref, wpb_ref, o_ref):
    h = h_ref[...]
    ga = jax.nn.sigmoid(_dot(h, wga_ref[...]))
    gb = jax.nn.sigmoid(_dot(h, wgb_ref[...]))
    o_ref[...] = (ga * _dot(oa_ref[...], wpa_ref[...]) + gb * _dot(ob_ref[...], wpb_ref[...])).astype(BF16)


def _merge(h, oa, ob, wga, wgb, wpa, wpb):
    t = h.shape[0]
    row = pl.BlockSpec((TM, D_MODEL), lambda j, i: (i, 0))
    col = pl.BlockSpec((D_MODEL, TN), lambda j, i: (0, j))
    return pl.pallas_call(
        _merge_kernel,
        grid=(D_MODEL // TN, t // TM),
        in_specs=[row, row, row, col, col, col, col],
        out_specs=pl.BlockSpec((TM, TN), lambda j, i: (i, j)),
        out_shape=jax.ShapeDtypeStruct((t, D_MODEL), BF16),
        compiler_params=_params(("parallel", "parallel"), VMEM_LIMIT),
        name="merge",
    )(h, oa, ob, wga, wgb, wpa, wpb)


def _outrouter_kernel(mg_ref, x_ref, wo_ref, g2_ref, rw_ref, rb_ref, tri_ref,
                      x1_ref, h2_ref, tope_ref, gate_ref, rank_ref, cnt_ref, carry_sc):
    tm = TM_OUT

    @pl.when(pl.program_id(0) == 0)
    def _():
        carry_sc[...] = jnp.zeros(carry_sc.shape, F32)

    x1 = x_ref[...] + _dot(mg_ref[...], wo_ref[...])
    x1_ref[...] = x1
    y = x1 * lax.rsqrt(jnp.mean(x1 * x1, axis=-1, keepdims=True) + NORM_EPS) * g2_ref[...]
    h2_ref[...] = y
    logits = _dot(y.astype(BF16), rw_ref[...]) + rb_ref[...]
    lane = lax.broadcasted_iota(I32, (tm, LANES), 1)
    lane_f = lane.astype(F32)
    work = jnp.where(lane < N_EXPERTS, logits, -jnp.inf)
    vals, idxs = [], []
    for _k in range(TOP_K):
        m = jnp.max(work, axis=-1, keepdims=True)
        idx = jnp.min(jnp.where(work == m, lane_f, float(LANES)), axis=-1, keepdims=True).astype(I32)
        vals.append(m)
        idxs.append(idx)
        work = jnp.where(lane == idx, -jnp.inf, work)
    es = [jnp.exp(v - vals[0]) for v in vals]
    denom = es[0] + es[1] + es[2] + es[3]
    cnt = jnp.zeros((tm, LANES), F32)
    for idx in idxs:
        cnt = cnt + (lane == idx).astype(F32)
    before = _dot(tri_ref[...], cnt.astype(BF16)) + carry_sc[...]
    tope = jnp.zeros((tm, LANES), I32)
    gate = jnp.zeros((tm, LANES), F32)
    rank = jnp.zeros((tm, LANES), F32)
    for k in range(TOP_K):
        tope = jnp.where(lane == k, idxs[k], tope)
        gate = jnp.where(lane == k, es[k] / denom, gate)
        rk = jnp.sum(jnp.where(lane == idxs[k], before, 0.0), axis=-1, keepdims=True)
        rank = jnp.where(lane == k, rk, rank)
    tope_ref[...] = tope
    gate_ref[...] = gate
    rank_ref[...] = rank.astype(I32)
    carry = carry_sc[...] + jnp.sum(cnt, axis=0, keepdims=True)
    carry_sc[...] = carry
    cnt_ref[...] = jnp.broadcast_to(carry, cnt_ref.shape).astype(I32)


def _outrouter(merged, x2, wo, g2, rw, rb, tri):
    t = x2.shape[0]
    row_bf = pl.BlockSpec((TM_OUT, D_MODEL), lambda i: (i, 0))
    lane_blk = pl.BlockSpec((TM_OUT, LANES), lambda i: (i, 0))
    return pl.pallas_call(
        _outrouter_kernel,
        grid=(t // TM_OUT,),
        in_specs=[
            row_bf,
            row_bf,
            pl.BlockSpec((D_MODEL, D_MODEL), lambda i: (0, 0)),
            pl.BlockSpec((1, D_MODEL), lambda i: (0, 0)),
            pl.BlockSpec((D_MODEL, LANES), lambda i: (0, 0)),
            pl.BlockSpec((1, LANES), lambda i: (0, 0)),
            pl.BlockSpec((TM_OUT, TM_OUT), lambda i: (0, 0)),
        ],
        out_specs=[row_bf, row_bf, lane_blk, lane_blk, lane_blk,
                   pl.BlockSpec((SUBLANES, LANES), lambda i: (0, 0))],
        out_shape=[
            jax.ShapeDtypeStruct((t, D_MODEL), F32),
            jax.ShapeDtypeStruct((t, D_MODEL), F32),
            jax.ShapeDtypeStruct((t, LANES), I32),
            jax.ShapeDtypeStruct((t, LANES), F32),
            jax.ShapeDtypeStruct((t, LANES), I32),
            jax.ShapeDtypeStruct((SUBLANES, LANES), I32),
        ],
        scratch_shapes=[pltpu.VMEM((1, LANES), F32)],
        compiler_params=_params(("arbitrary",), VMEM_LIMIT),
        name="outrouter",
    )(merged, x2, wo, g2, rw, rb, tri)


def _row_copy(src_hbm, dst, src_row, dst_row, sem):
    return pltpu.make_async_copy(src_hbm.at[pl.ds(src_row, 1)], dst.at[pl.ds(dst_row, 1)], sem)


def _issue_rows(n_rows, copy_of_row, n_queues=1):
    for r in range(n_rows):
        copy_of_row(r).start(priority=r % n_queues)


def _drain_rows(n_rows, any_row_copy):
    def body(r, carry):
        any_row_copy.wait()
        return carry

    lax.fori_loop(0, n_rows, body, 0, unroll=DMA_UNROLL)


def _experts_kernel(ce_ref, na_ref, tokc_ref, tokn_ref, h_hbm, wg_hbm, bg_ref, wu_hbm, bu_ref, wd_hbm, bd_ref,
                    y_ref, xbuf, sem, wg_sc, wu_sc, wd_sc, sg_sc, su_sc, sd_sc, wsem):
    c = pl.program_id(0)
    n_act = na_ref[0]
    slot = c & 1
    last_c = pl.num_programs(0) - 1
    e = ce_ref[c]
    e_next = ce_ref[jnp.minimum(c + 1, last_c)]
    is_first = jnp.logical_or(c == 0, ce_ref[jnp.maximum(c - 1, 0)] != e)
    next_switch = jnp.logical_and(c + 1 < n_act, e_next != e)

    def gather(tok_ref, s_):
        _issue_rows(MOE_ROWS, lambda r: _row_copy(h_hbm, xbuf.at[s_], tok_ref[0, 0, r], r, sem.at[s_]))

    def stage_copies(ex, s, sl):
        span = pl.ds(s * W_STAGE, W_STAGE)
        full = pl.ds(0, D_MODEL)
        return (pltpu.make_async_copy(wg_hbm.at[ex, full, span], sg_sc.at[sl], wsem.at[sl, 0]),
                pltpu.make_async_copy(wu_hbm.at[ex, full, span], su_sc.at[sl], wsem.at[sl, 1]),
                pltpu.make_async_copy(wd_hbm.at[ex, span, full], sd_sc.at[sl], wsem.at[sl, 2]))

    def start_stage(ex, s, sl):
        for cp in stage_copies(ex, s, sl):
            cp.start(priority=1)

    def prefetch_next():
        for s in range(N_WSLOTS):
            start_stage(e_next, s, s)

    def ffn_chunk(x, acc, lo, width):
        cols = slice(lo, lo + width)
        g = _dot(x, wg_sc[:, cols]) + bg_ref[0, :, cols]
        u = _dot(x, wu_sc[:, cols]) + bu_ref[0, :, cols]
        g = jnp.minimum(g, SWIGLU_LIMIT)
        u = jnp.clip(u, -SWIGLU_LIMIT, SWIGLU_LIMIT)
        act = (u + 1.0) * (g * jax.nn.sigmoid(SWIGLU_ALPHA * g))
        return acc + _dot(act.astype(BF16), wd_sc[cols, :])

    @pl.when(jnp.logical_and(c == 0, n_act > 0))
    def _():
        gather(tokc_ref, 0)
        for s in range(N_WSLOTS):
            start_stage(e, s, s)

    @pl.when(c + 1 < n_act)
    def _():
        gather(tokn_ref, 1 - slot)

    @pl.when(c < n_act)
    def _():
        _drain_rows(MOE_ROWS, _row_copy(h_hbm, xbuf.at[slot], 0, 0, sem.at[slot]))
        x = xbuf[slot].astype(BF16)

        @pl.when(is_first)
        def _():
            acc = jnp.zeros((MOE_ROWS, D_MODEL), F32)
            for s in range(D_FF // W_STAGE):
                sl = s % N_WSLOTS
                span = slice(s * W_STAGE, (s + 1) * W_STAGE)
                for cp in stage_copies(e, s, sl):
                    cp.wait()
                wg_sc[:, span] = sg_sc[sl].astype(BF16)
                wu_sc[:, span] = su_sc[sl].astype(BF16)
                wd_sc[span, :] = sd_sc[sl].astype(BF16)
                if s + N_WSLOTS < D_FF // W_STAGE:
                    start_stage(e, s + N_WSLOTS, sl)
                acc = ffn_chunk(x, acc, s * W_STAGE, W_STAGE)
            y_ref[...] = acc + bd_ref[0]

            @pl.when(next_switch)
            def _():
                prefetch_next()

        @pl.when(jnp.logical_not(is_first))
        def _():
            @pl.when(next_switch)
            def _():
                prefetch_next()

            acc = jnp.zeros((MOE_ROWS, D_MODEL), F32)
            for f in range(D_FF // FF_CHUNK):
                acc = ffn_chunk(x, acc, f * FF_CHUNK, FF_CHUNK)
            y_ref[...] = acc + bd_ref[0]

    @pl.when(c >= n_act)
    def _():
        y_ref[...] = jnp.zeros(y_ref.shape, y_ref.dtype)


def _experts(chunk_e, n_active, row_tok, h2, wg, bg, wu, bu, wd, bd, n_chunks):
    def wspec():
        return pl.BlockSpec(memory_space=pl.ANY)

    def bspec():
        return pl.BlockSpec((1, 1, D_FF), lambda c, ce, na: (ce[c], 0, 0))

    tok3 = row_tok.reshape(n_chunks, 1, MOE_ROWS)
    grid_spec = pltpu.PrefetchScalarGridSpec(
        num_scalar_prefetch=2,
        grid=(n_chunks,),
        in_specs=[
            pl.BlockSpec((1, 1, MOE_ROWS), lambda c, ce, na: (c, 0, 0), memory_space=pltpu.SMEM),
            pl.BlockSpec((1, 1, MOE_ROWS), lambda c, ce, na: (jnp.minimum(c + 1, n_chunks - 1), 0, 0),
                         memory_space=pltpu.SMEM),
            pl.BlockSpec(memory_space=pl.ANY),
            wspec(), bspec(),
            wspec(), bspec(),
            wspec(), bspec(),
        ],
        out_specs=pl.BlockSpec((MOE_ROWS, D_MODEL), lambda c, ce, na: (c, 0)),
        scratch_shapes=[
            pltpu.VMEM((2, MOE_ROWS, D_MODEL), F32),
            pltpu.SemaphoreType.DMA((2,)),
            pltpu.VMEM((D_MODEL, D_FF), BF16),
            pltpu.VMEM((D_MODEL, D_FF), BF16),
            pltpu.VMEM((D_FF, D_MODEL), BF16),
            pltpu.VMEM((N_WSLOTS, D_MODEL, W_STAGE), F32),
            pltpu.VMEM((N_WSLOTS, D_MODEL, W_STAGE), F32),
            pltpu.VMEM((N_WSLOTS, W_STAGE, D_MODEL), F32),
            pltpu.SemaphoreType.DMA((N_WSLOTS, 3)),
        ],
    )
    return pl.pallas_call(
        _experts_kernel,
        grid_spec=grid_spec,
        out_shape=jax.ShapeDtypeStruct((n_chunks * MOE_ROWS, D_MODEL), F32),
        compiler_params=_params(("arbitrary",), VMEM_LIMIT_EXPERTS),
        name="experts",
    )(chunk_e, n_active, tok3, tok3, h2, wg, bg, wu, bu, wd, bd)


def _combine_kernel(destc_ref, destn_ref, gate_ref, x1_ref, gf_ref, y_hbm, o_ref, ybuf, sem):
    i = pl.program_id(0)
    slot = i & 1
    n = TB_COMBINE * TOP_K

    def gather(dest_ref, s_):
        _issue_rows(n, lambda r: _row_copy(y_hbm, ybuf.at[s_, r & (TOP_K - 1)], dest_ref[0, 0, r],
                                           r >> TOP_K_LOG2, sem.at[s_]), n_queues=2)

    @pl.when(i == 0)
    def _():
        gather(destc_ref, 0)

    @pl.when(i + 1 < pl.num_programs(0))
    def _():
        gather(destn_ref, 1 - slot)

    _drain_rows(n, _row_copy(y_hbm, ybuf.at[slot, 0], 0, 0, sem.at[slot]))
    gate = gate_ref[...]
    out = x1_ref[...]
    for k in range(TOP_K):
        out = out + gate[:, k:k + 1] * ybuf[slot, k]
    o_ref[...] = out * lax.rsqrt(jnp.mean(out * out, axis=-1, keepdims=True) + NORM_EPS) * gf_ref[...]


def _combine(dest, gate, x1, gf, y_rows):
    t = x1.shape[0]
    nt = t // TB_COMBINE
    dest3 = dest.reshape(nt, 1, TB_COMBINE * TOP_K)
    return pl.pallas_call(
        _combine_kernel,
        grid=(nt,),
        in_specs=[
            pl.BlockSpec((1, 1, TB_COMBINE * TOP_K), lambda i: (i, 0, 0), memory_space=pltpu.SMEM),
            pl.BlockSpec((1, 1, TB_COMBINE * TOP_K), lambda i: (jnp.minimum(i + 1, nt - 1), 0, 0),
                         memory_space=pltpu.SMEM),
            pl.BlockSpec((TB_COMBINE, LANES), lambda i: (i, 0)),
            pl.BlockSpec((TB_COMBINE, D_MODEL), lambda i: (i, 0)),
            pl.BlockSpec((1, D_MODEL), lambda i: (0, 0)),
            pl.BlockSpec(memory_space=pl.ANY),
        ],
        out_specs=pl.BlockSpec((TB_COMBINE, D_MODEL), lambda i: (i, 0)),
        out_shape=jax.ShapeDtypeStruct((t, D_MODEL), F32),
        scratch_shapes=[
            pltpu.VMEM((2, TOP_K, TB_COMBINE, D_MODEL), F32),
            pltpu.SemaphoreType.DMA((2,)),
        ],
        compiler_params=_params(("arbitrary",)),
        name="combine",
    )(dest3, dest3, gate, x1, gf, y_rows)


def _rope_tables(s):
    pos = jnp.arange(s, dtype=F32)
    inv = ROPE_THETA ** (-jnp.arange(0, HEAD_DIM, 2, dtype=F32) / HEAD_DIM)
    ang = pos[:, None] * inv[None, :]
    ang = jnp.concatenate([ang, ang], axis=-1)
    sign = jnp.where(jnp.arange(HEAD_DIM) < HEAD_DIM // 2, -1.0, 1.0).astype(F32)
    return jnp.cos(ang), jnp.sin(ang) * sign[None, :]


def _cmp_to_sel_t(nc, n_sel):
    cs = np.arange(nc) * CMP_STRIDE
    ss = np.arange(n_sel) * SEL_BLOCK
    ov = np.minimum(cs[:, None] + CMP_BLOCK, ss[None, :] + SEL_BLOCK) - np.maximum(cs[:, None], ss[None, :])
    m = np.clip(ov, 0, None) / CMP_BLOCK
    m[nc - 1, :] = 0.0
    out = np.zeros((LANES, nc), np.float32)
    out[:n_sel] = m.T
    return jnp.asarray(out, dtype=BF16)


def _block_expand(s):
    key = np.arange(s).reshape(s // TK_SEL, TK_SEL, 1)
    blk = np.arange(LANES).reshape(1, 1, LANES)
    return jnp.asarray((key // SEL_BLOCK == blk).astype(np.float32), dtype=BF16)


def kernel(x, norm1_g, w_in, cmp_pe_k, cmp_pe_v, cmp_k_w1, cmp_k_w2, cmp_v_w1, cmp_v_w2, lru_conv_w, lru_conv_b, lru_wa, lru_ba, lru_wi, lru_bi, lru_lambda, w_br_attn, w_br_lru, w_out, norm2_g, router_w, router_b, exp_w_gate, exp_b_gate, exp_w_up, exp_b_up, exp_w_down, exp_b_down, normf_g):
    b, s, d = x.shape
    assert d == D_MODEL and norm1_g.shape[0] == 1, "single-layer block of width D_MODEL"
    assert s % TM == 0 and s % TS_LRU == 0 and s >= WIN_KEYS and s // SEL_BLOCK <= LANES
    assert s % (2 * TK_SEL) == 0, "the selected branch walks key tiles in pairs"
    t = b * s
    x2 = x.reshape(t, d)

    w_in0 = w_in[0]
    o = 0
    wq = w_in0[:, o:o + Q_W].astype(BF16); o += Q_W
    wkv = w_in0[:, o:o + KV_KINDS * KV_W].reshape(d, KV_KINDS, KV_W); o += KV_KINDS * KV_W
    wk4 = wkv[:, (0, 1, 2, 4), :].reshape(d, ROW_KINDS * KV_W).astype(BF16)
    wv2 = wkv[:, (3, 5), :].reshape(d, 2 * KV_W).astype(BF16)
    n_g = 3 * NSA_Q_HEADS
    wg_nsa = w_in0[:, o:o + n_g].reshape(d, NSA_KV_GROUPS, 3 * NSA_GROUP); o += n_g
    wg_nsa = jnp.pad(wg_nsa, ((0, 0), (0, 0), (0, LANES - 3 * NSA_GROUP))).reshape(d, NSA_KV_GROUPS * LANES).astype(BF16)
    wlxy = w_in0[:, o:o + 2 * LRU_WIDTH].astype(BF16); o += 2 * LRU_WIDTH
    wga = w_in0[:, o:o + D_MODEL].astype(BF16); o += D_MODEL
    wgb = w_in0[:, o:o + D_MODEL].astype(BF16); o += D_MODEL

    cos, sin_signed = _rope_tables(s)
    nc = s // CMP_STRIDE
    n_sel = s // SEL_BLOCK

    h, gates = _norm1(x2, norm1_g, wg_nsa)
    q_raw, q_rot = _qproj(h, wq, cos, sin_signed, s)
    kv4 = _kvproj(h, wk4, cos, sin_signed, b, s)
    vt2 = _vtproj(h, wv2, b, s)
    lxy = _proj(h, wlxy, "lruproj")

    cw1 = jnp.stack([cmp_k_w1[0], cmp_v_w1[0]]).astype(BF16)
    cw2 = jnp.stack([cmp_k_w2[0], cmp_v_w2[0]]).astype(BF16)
    pe = jnp.stack([cmp_pe_k[0], cmp_pe_v[0]]).reshape(2, 1, CMP_BLOCK * HEAD_DIM)
    pe = jnp.pad(pe, ((0, 0), (0, SUBLANES - 1), (0, 0))).astype(BF16)
    kvc = _cmpkv(kv4, cw1, cw2, pe, b, s)
    ocmp, msk = _cmpattn(q_raw, kvc, _cmp_to_sel_t(nc, n_sel), b, s)
    o_a = _selwin(q_rot, kv4, vt2, msk, _block_expand(s), ocmp, gates, b, s)

    o_b = _lru(lxy, lru_conv_w[0], lru_conv_b, lru_wa[0].astype(BF16), lru_ba[0].reshape(1, LRU_WIDTH),
               lru_wi[0].astype(BF16), lru_bi[0].reshape(1, LRU_WIDTH), lru_lambda, b, s)

    merged = _merge(h, o_a.reshape(t, Q_W), o_b.reshape(t, LRU_WIDTH), wga, wgb,
                    w_br_attn[0].astype(BF16), w_br_lru[0].astype(BF16))

    rw = jnp.pad(router_w[0], ((0, 0), (0, LANES - N_EXPERTS))).astype(BF16)
    rb = jnp.pad(router_b, ((0, 0), (0, LANES - N_EXPERTS)))
    tri = jnp.asarray(np.tril(np.ones((TM_OUT, TM_OUT), np.float32), -1), dtype=BF16)
    x1, h2, tope, gate, rank, cnt = _outrouter(merged, x2, w_out[0].astype(BF16), norm2_g, rw, rb, tri)

    counts = cnt[0, :N_EXPERTS]
    padded = (counts + MOE_ROWS - 1) // MOE_ROWS * MOE_ROWS
    pend = jnp.cumsum(padded)
    pstart = pend - padded
    n_rows = t * TOP_K + N_EXPERTS * MOE_ROWS
    n_chunks = n_rows // MOE_ROWS
    dest = (pstart[tope[:, :TOP_K]] + rank[:, :TOP_K]).astype(I32)
    tok_of = jnp.broadcast_to(jnp.arange(t, dtype=I32)[:, None], (t, TOP_K))
    row_tok = jnp.zeros((n_rows,), I32).at[dest.reshape(-1)].set(tok_of.reshape(-1))
    chunk_e = jnp.minimum(jnp.sum((jnp.arange(n_chunks) * MOE_ROWS)[:, None] >= pend[None, :], axis=1),
                          N_EXPERTS - 1).astype(I32)
    n_active = (pend[-1:] // MOE_ROWS).astype(I32)

    y_rows = _experts(chunk_e, n_active, row_tok, h2,
                      exp_w_gate.reshape(N_EXPERTS, D_MODEL, D_FF), exp_b_gate[0].reshape(N_EXPERTS, 1, D_FF),
                      exp_w_up.reshape(N_EXPERTS, D_MODEL, D_FF), exp_b_up[0].reshape(N_EXPERTS, 1, D_FF),
                      exp_w_down.reshape(N_EXPERTS, D_FF, D_MODEL), exp_b_down[0].reshape(N_EXPERTS, 1, D_MODEL),
                      n_chunks)
    out = _combine(dest, gate, x1, normf_g.reshape(1, D_MODEL), y_rows)
    return out.reshape(b, s, d)
```

```python
import functools

import numpy as np
import jax
import jax.numpy as jnp
from jax import lax
from jax.experimental import pallas as pl
from jax.experimental.pallas import tpu as pltpu

F32 = jnp.float32
BF16 = jnp.bfloat16
I32 = jnp.int32

D_MODEL = 2048
HEAD_DIM = 128
NSA_Q_HEADS = 16
NSA_KV_GROUPS = 4
NSA_GROUP = NSA_Q_HEADS // NSA_KV_GROUPS
CMP_BLOCK = 32
CMP_STRIDE = 16
CMP_HIDDEN = 256
SEL_BLOCK = 64
SEL_BLOCK_LOG2 = 6
SEL_TOPK = 16
WINDOW = 512
ROPE_THETA = 10000.0
LRU_WIDTH = 2048
LRU_BLOCKS = 16
LRU_BLOCK_W = LRU_WIDTH // LRU_BLOCKS
CONV_WIDTH = 4
LRU_C = 8.0
N_EXPERTS = 32
TOP_K = 4
TOP_K_LOG2 = 2
D_FF = 2048
SWIGLU_ALPHA = 1.702
SWIGLU_LIMIT = 7.0
NORM_EPS = 1e-6
NEG_INF = -1e30
FORCE_SCORE = 1e6

Q_W = NSA_Q_HEADS * HEAD_DIM
KV_W = NSA_KV_GROUPS * HEAD_DIM
GROUP_W = NSA_GROUP * HEAD_DIM
SCALE = HEAD_DIM ** -0.5
LOG2_E = 1.4426950408889634
LANES = 128
SUBLANES = 8

TM = 1024
TN = 512
TN_WIDE = 1024
TQ_CMP = 256
TQ = 256
TQ_LOG2 = 8
TK_SEL = 256
WIN_KEYS = WINDOW + TQ
VT_ROWS = HEAD_DIM + 16
TS_LRU = 256
CW_LRU = 512
TM_OUT = 256
MOE_ROWS = 256
FF_CHUNK = 512
TB_COMBINE = 128
DMA_UNROLL = 8
W_STAGE = 256
N_WSLOTS = 2
VMEM_LIMIT = 56 * 1024 * 1024
VMEM_LIMIT_EXPERTS = 60 * 1024 * 1024


def _params(sem, vmem=None):
    return pltpu.CompilerParams(dimension_semantics=sem, vmem_limit_bytes=vmem)


def _dot(a, b):
    return jnp.dot(a, b, preferred_element_type=F32)


def _dot_nt(a, b, precision=None):
    return lax.dot_general(a, b, (((1,), (1,)), ((), ())), precision=precision, preferred_element_type=F32)


def _norm1_kernel(x_ref, g_ref, wg_ref, h_ref, gate_ref):
    x = x_ref[...]
    y = x * lax.rsqrt(jnp.mean(x * x, axis=-1, keepdims=True) + NORM_EPS) * g_ref[...]
    hb = y.astype(BF16)
    h_ref[...] = hb
    gate_ref[...] = jax.nn.sigmoid(_dot(hb, wg_ref[...]))


def _norm1(x2, g, wg):
    t = x2.shape[0]
    return pl.pallas_call(
        _norm1_kernel,
        grid=(t // TM,),
        in_specs=[
            pl.BlockSpec((TM, D_MODEL), lambda i: (i, 0)),
            pl.BlockSpec((1, D_MODEL), lambda i: (0, 0)),
            pl.BlockSpec((D_MODEL, NSA_KV_GROUPS * LANES), lambda i: (0, 0)),
        ],
        out_specs=[
            pl.BlockSpec((TM, D_MODEL), lambda i: (i, 0)),
            pl.BlockSpec((TM, NSA_KV_GROUPS * LANES), lambda i: (i, 0)),
        ],
        out_shape=[
            jax.ShapeDtypeStruct((t, D_MODEL), BF16),
            jax.ShapeDtypeStruct((t, NSA_KV_GROUPS * LANES), F32),
        ],
        compiler_params=_params(("parallel",)),
        name="norm1",
    )(x2, g, wg)


def _rope(x, cos, sin_signed):
    return x * cos + pltpu.roll(x, HEAD_DIM // 2, axis=1) * sin_signed


def _qproj_kernel(h_ref, w_ref, cos_ref, sin_ref, qraw_ref, qrot_ref):
    acc = _dot(h_ref[...], w_ref[...])
    qraw_ref[...] = acc.astype(BF16)
    cos = cos_ref[...]
    sin = sin_ref[...]
    for hh in range(TN_WIDE // HEAD_DIM):
        sl = slice(hh * HEAD_DIM, (hh + 1) * HEAD_DIM)
        qrot_ref[:, sl] = _rope(acc[:, sl], cos, sin).astype(BF16)


def _qproj(h, wq, cos, sin, s):
    t = h.shape[0]
    ns = s // TM
    return pl.pallas_call(
        _qproj_kernel,
        grid=(Q_W // TN_WIDE, t // TM),
        in_specs=[
            pl.BlockSpec((TM, D_MODEL), lambda j, i: (i, 0)),
            pl.BlockSpec((D_MODEL, TN_WIDE), lambda j, i: (0, j)),
            pl.BlockSpec((TM, HEAD_DIM), lambda j, i: (i % ns, 0)),
            pl.BlockSpec((TM, HEAD_DIM), lambda j, i: (i % ns, 0)),
        ],
        out_specs=[
            pl.BlockSpec((TM, TN_WIDE), lambda j, i: (i, j)),
            pl.BlockSpec((TM, TN_WIDE), lambda j, i: (i, j)),
        ],
        out_shape=[jax.ShapeDtypeStruct((t, Q_W), BF16), jax.ShapeDtypeStruct((t, Q_W), BF16)],
        compiler_params=_params(("parallel", "parallel")),
        name="qproj",
    )(h, wq, cos, sin)


KV_KINDS = 6
ROW_KINDS = 4
ROW_K_SEL = 2
ROW_K_WIN = 3


def _kvproj_kernel(h_ref, w_ref, cos_ref, sin_ref, out_ref):
    kind = pl.program_id(0)
    acc = _dot(h_ref[...], w_ref[...])
    cos = cos_ref[...]
    sin = sin_ref[...]
    is_rope = jnp.logical_or(kind == ROW_K_SEL, kind == ROW_K_WIN)
    for g in range(NSA_KV_GROUPS):
        xh = acc[:, g * HEAD_DIM:(g + 1) * HEAD_DIM]
        out_ref[0, 0, g] = jnp.where(is_rope, _rope(xh, cos, sin), xh).astype(BF16)


def _kvproj(h, wk4, cos, sin, b, s):
    t = h.shape[0]
    ns = s // TM
    return pl.pallas_call(
        _kvproj_kernel,
        grid=(ROW_KINDS, t // TM),
        in_specs=[
            pl.BlockSpec((TM, D_MODEL), lambda k, i: (i, 0)),
            pl.BlockSpec((D_MODEL, KV_W), lambda k, i: (0, k)),
            pl.BlockSpec((TM, HEAD_DIM), lambda k, i: (i % ns, 0)),
            pl.BlockSpec((TM, HEAD_DIM), lambda k, i: (i % ns, 0)),
        ],
        out_specs=pl.BlockSpec((1, 1, NSA_KV_GROUPS, TM, HEAD_DIM), lambda k, i: (k, i // ns, 0, i % ns, 0)),
        out_shape=jax.ShapeDtypeStruct((ROW_KINDS, b, NSA_KV_GROUPS, s, HEAD_DIM), BF16),
        compiler_params=_params(("parallel", "parallel")),
        name="kvproj",
    )(h, wk4, cos, sin)


def _vtproj_kernel(h_ref, w_ref, out_ref):
    acc = _dot(h_ref[...], w_ref[...])
    sub = lax.broadcasted_iota(I32, (VT_ROWS - HEAD_DIM, TK_SEL), 0)
    ones_row = jnp.where(sub == 0, 1.0, 0.0).astype(BF16)
    for g in range(NSA_KV_GROUPS):
        for tt in range(TM // TK_SEL):
            tile = acc[tt * TK_SEL:(tt + 1) * TK_SEL, g * HEAD_DIM:(g + 1) * HEAD_DIM]
            out_ref[0, 0, g, tt, :HEAD_DIM, :] = tile.T.astype(BF16)
            out_ref[0, 0, g, tt, HEAD_DIM:, :] = ones_row


def _vtproj(h, wv2, b, s):
    t = h.shape[0]
    ns = s // TM
    nt = TM // TK_SEL
    return pl.pallas_call(
        _vtproj_kernel,
        grid=(2, t // TM),
        in_specs=[
            pl.BlockSpec((TM, D_MODEL), lambda k, i: (i, 0)),
            pl.BlockSpec((D_MODEL, KV_W), lambda k, i: (0, k)),
        ],
        out_specs=pl.BlockSpec((1, 1, NSA_KV_GROUPS, nt, VT_ROWS, TK_SEL),
                               lambda k, i: (k, i // ns, 0, i % ns, 0, 0)),
        out_shape=jax.ShapeDtypeStruct((2, b, NSA_KV_GROUPS, s // TK_SEL, VT_ROWS, TK_SEL), BF16),
        compiler_params=_params(("parallel", "parallel")),
        name="vtproj",
    )(h, wv2)


def _proj_kernel(h_ref, w_ref, o_ref):
    o_ref[...] = _dot(h_ref[...], w_ref[...]).astype(o_ref.dtype)


def _proj(h, w, name):
    t = h.shape[0]
    n = w.shape[1]
    return pl.pallas_call(
        _proj_kernel,
        grid=(n // TN_WIDE, t // TM),
        in_specs=[
            pl.BlockSpec((TM, D_MODEL), lambda j, i: (i, 0)),
            pl.BlockSpec((D_MODEL, TN_WIDE), lambda j, i: (0, j)),
        ],
        out_specs=pl.BlockSpec((TM, TN_WIDE), lambda j, i: (i, j)),
        out_shape=jax.ShapeDtypeStruct((t, n), BF16),
        compiler_params=_params(("parallel", "parallel")),
        name=name,
    )(h, w)


def _cmpkv_kernel(c_ref, w1_ref, w2_ref, pe_ref, o_ref):
    c = c_ref[0, 0, 0]
    half = CMP_STRIDE * HEAD_DIM
    top = _dot(c, w1_ref[0, :half, :])
    bot = _dot(c, w1_ref[0, half:, :])
    n_rows = top.shape[0]
    bot_next = pltpu.roll(bot, n_rows - 1, axis=0)
    bias = _dot(pe_ref[0], w1_ref[0])[0:1, :]
    hid = jax.nn.gelu(top + bot_next + bias)
    o_ref[0, 0, 0] = _dot(hid.astype(BF16), w2_ref[0]).astype(BF16)


def _cmpkv(kv4, w1, w2, pe, b, s):
    nc = s // CMP_STRIDE
    kvc = kv4[:2].reshape(2, b, NSA_KV_GROUPS, nc, CMP_STRIDE * HEAD_DIM)
    return pl.pallas_call(
        _cmpkv_kernel,
        grid=(2, b, NSA_KV_GROUPS),
        in_specs=[
            pl.BlockSpec((1, 1, 1, nc, CMP_STRIDE * HEAD_DIM), lambda k, bi, g: (k, bi, g, 0, 0)),
            pl.BlockSpec((1, CMP_BLOCK * HEAD_DIM, CMP_HIDDEN), lambda k, bi, g: (k, 0, 0)),
            pl.BlockSpec((1, CMP_HIDDEN, HEAD_DIM), lambda k, bi, g: (k, 0, 0)),
            pl.BlockSpec((1, SUBLANES, CMP_BLOCK * HEAD_DIM), lambda k, bi, g: (k, 0, 0)),
        ],
        out_specs=pl.BlockSpec((1, 1, 1, nc, HEAD_DIM), lambda k, bi, g: (k, bi, g, 0, 0)),
        out_shape=jax.ShapeDtypeStruct((2, b, NSA_KV_GROUPS, nc, HEAD_DIM), BF16),
        compiler_params=_params(("parallel", "parallel", "parallel")),
        name="cmpkv",
    )(kvc, w1, w2, pe)


def _cmpattn_kernel(q_ref, kc_ref, vc_ref, mt_ref, ocmp_ref, msk_ref, *, n_sel):
    i = pl.program_id(2)
    tq = TQ_CMP
    q = q_ref[0]
    kc = kc_ref[0, 0, 0]
    vc = vc_ref[0, 0, 0]
    nc = kc.shape[0]
    t_row = i * tq + lax.broadcasted_iota(I32, (tq, 1), 0)
    n_lane = lax.broadcasted_iota(I32, (tq, nc), 1)
    cmask = (n_lane * CMP_STRIDE + (CMP_BLOCK - 1)) <= t_row
    psum = jnp.zeros((tq, nc), F32)
    for hh in range(NSA_GROUP):
        sl = slice(hh * HEAD_DIM, (hh + 1) * HEAD_DIM)
        s = jnp.where(cmask, _dot_nt(q[:, sl], kc), NEG_INF)
        e = jnp.exp2((s - jnp.max(s, axis=-1, keepdims=True)) * (SCALE * LOG2_E))
        p = jnp.where(cmask, e * (1.0 / jnp.sum(e, axis=-1, keepdims=True)), 0.0)
        ocmp_ref[0, :, sl] = _dot(p.astype(BF16), vc).astype(BF16)
        psum = psum + p
    imp_t = _dot_nt(mt_ref[...], psum.astype(BF16))
    imp_t = imp_t[:n_sel]
    jb = lax.broadcasted_iota(I32, (n_sel, tq), 0)
    cur = (i * tq + lax.broadcasted_iota(I32, (n_sel, tq), 1)) >> SEL_BLOCK_LOG2
    forced = (jb == 0) | (jb == cur) | (jb == cur - 1)
    valid = jb <= cur
    score = jnp.where(forced, FORCE_SCORE, jnp.where(valid, imp_t, NEG_INF))
    pad = jnp.zeros((LANES - n_sel, tq), F32)
    needs_rank = (i + 1) * tq > SEL_TOPK * SEL_BLOCK

    @pl.when(needs_rank)
    def _():
        n_grp = n_sel // SUBLANES
        grp = [score[gi * SUBLANES:(gi + 1) * SUBLANES] for gi in range(n_grp)]
        jb_grp = lax.broadcasted_iota(I32, (SUBLANES, tq), 0)
        ranks = [jnp.zeros((SUBLANES, tq), F32) for _ in range(n_grp)]
        for jp in range(n_sel):
            row = score[jp:jp + 1, :]
            for gi in range(n_grp):
                lo = gi * SUBLANES
                if lo > jp:
                    beats = row >= grp[gi]
                elif lo + SUBLANES - 1 <= jp:
                    beats = row > grp[gi]
                else:
                    beats = (row > grp[gi]) | ((row == grp[gi]) & (jb_grp > jp - lo))
                ranks[gi] = ranks[gi] + jnp.where(beats, 1.0, 0.0)
        rank = jnp.concatenate(ranks, axis=0)
        member = (valid & (rank < float(SEL_TOPK))).astype(F32)
        msk_ref[0, 0] = jnp.concatenate([member, pad], axis=0).astype(BF16)

    @pl.when(jnp.logical_not(needs_rank))
    def _():
        msk_ref[0, 0] = jnp.concatenate([valid.astype(F32), pad], axis=0).astype(BF16)


def _cmpattn(q_raw, kvc, mt, b, s):
    nc = s // CMP_STRIDE
    n_sel = s // SEL_BLOCK
    q3 = q_raw.reshape(b, s, Q_W)
    return pl.pallas_call(
        functools.partial(_cmpattn_kernel, n_sel=n_sel),
        grid=(b, NSA_KV_GROUPS, s // TQ_CMP),
        in_specs=[
            pl.BlockSpec((1, TQ_CMP, GROUP_W), lambda bi, g, i: (bi, i, g)),
            pl.BlockSpec((1, 1, 1, nc, HEAD_DIM), lambda bi, g, i: (0, bi, g, 0, 0)),
            pl.BlockSpec((1, 1, 1, nc, HEAD_DIM), lambda bi, g, i: (1, bi, g, 0, 0)),
            pl.BlockSpec((LANES, nc), lambda bi, g, i: (0, 0)),
        ],
        out_specs=[
            pl.BlockSpec((1, TQ_CMP, GROUP_W), lambda bi, g, i: (bi, i, g)),
            pl.BlockSpec((1, 1, LANES, TQ_CMP), lambda bi, g, i: (bi, g, 0, i)),
        ],
        out_shape=[
            jax.ShapeDtypeStruct((b, s, Q_W), BF16),
            jax.ShapeDtypeStruct((b, NSA_KV_GROUPS, LANES, s), BF16),
        ],
        compiler_params=_params(("parallel", "parallel", "parallel")),
        name="cmpattn",
    )(q3, kvc, kvc, mt)


def _selwin_kernel(q_ref, ks_ref, vs_ref, kw_ref, vw_ref, msk_ref, emat_ref, ocmp_ref, gate_ref, o_ref,
                   acc_sc, accw_sc, sa_sc, sb_sc, ba_sc, bb_sc):
    assert TQ == TK_SEL, "the window walk assumes key tile n covers the queries of grid step n"
    i = pl.program_id(2)
    tq = TQ
    q = q_ref[0]
    heads = [slice(hh * tq, (hh + 1) * tq) for hh in range(NSA_GROUP)]
    q_t = jnp.concatenate(
        [q[:, hh * HEAD_DIM:(hh + 1) * HEAD_DIM].astype(F32).T.astype(BF16) for hh in range(NSA_GROUP)], axis=1)
    t_lane = i * tq + lax.broadcasted_iota(I32, (1, tq), 1)

    c_exp = SCALE * LOG2_E

    def make_branch(lhs_of, rhs, v_ref, acc_ref, visible):
        def scores(j, s_ref, b_ref=None):
            off = pl.multiple_of(j * TK_SEL, TK_SEL)
            s_ref[...] = _dot(lhs_of(j, off), rhs)
            if b_ref is not None:
                kpos = off + lax.broadcasted_iota(I32, (TK_SEL, tq), 0)
                b_ref[...] = jnp.where(visible(kpos), 0.0, NEG_INF)

        def absorb(j, s_ref, m_old, b_ref=None):
            v_t = v_ref[0, 0, 0, j]
            m_new = []
            for hh in range(NSA_GROUP):
                sh = s_ref[:, heads[hh]]
                if b_ref is not None:
                    sh = sh + b_ref[...]
                mh = jnp.maximum(m_old[hh], jnp.max(sh, axis=0, keepdims=True))
                alpha = jnp.exp2((m_old[hh] - mh) * c_exp)
                p = jnp.exp2((sh - mh) * c_exp).astype(BF16)
                acc_ref[:, heads[hh]] = acc_ref[:, heads[hh]] * alpha + _dot(v_t, p)
                m_new.append(mh)
            return tuple(m_new)

        return scores, absorb

    m0 = tuple(jnp.full((1, tq), NEG_INF, F32) for _ in range(NSA_GROUP))

    not_picked = ((msk_ref[0, 0].astype(F32) - 1.0) * (-NEG_INF)).astype(BF16)
    rhs_sel = jnp.concatenate([q_t, jnp.concatenate([not_picked] * NSA_GROUP, axis=1)], axis=0)

    def sel_lhs(j, off):
        return jnp.concatenate([ks_ref[0, 0, 0, pl.ds(off, TK_SEL), :], emat_ref[j]], axis=1)

    def causal(kpos):
        return kpos <= t_lane

    sel_scores, sel_absorb = make_branch(sel_lhs, rhs_sel, vs_ref, acc_sc, causal)
    acc_sc[...] = jnp.zeros(acc_sc.shape, F32)

    n_full = i >> 1
    sel_scores(0, sa_sc)

    def pair(jj, carry):
        j0 = 2 * jj
        sel_scores(j0 + 1, sb_sc)
        carry = sel_absorb(j0, sa_sc, carry)
        sel_scores(j0 + 2, sa_sc)
        return sel_absorb(j0 + 1, sb_sc, carry)

    m_sel = lax.fori_loop(0, n_full, pair, m0)
    kpos_a = 2 * n_full * TK_SEL + lax.broadcasted_iota(I32, (TK_SEL, tq), 0)
    ba_sc[...] = jnp.where(causal(kpos_a), 0.0, NEG_INF)
    sel_scores(2 * n_full + 1, sb_sc, bb_sc)
    m_sel = sel_absorb(2 * n_full, sa_sc, m_sel, ba_sc)
    sel_absorb(2 * n_full + 1, sb_sc, m_sel, bb_sc)

    def win_visible(kpos):
        return (kpos <= t_lane) & (kpos > t_lane - WINDOW)

    def win_lhs(j, off):
        return kw_ref[0, 0, 0, pl.ds(off, TK_SEL), :]

    win_scores, win_absorb = make_branch(win_lhs, q_t, vw_ref, accw_sc, win_visible)
    accw_sc[...] = jnp.zeros(accw_sc.shape, F32)
    n_back = WINDOW // TK_SEL
    tiles = [i] + [jnp.where(i >= kb, i - kb, kb) for kb in range(1, n_back + 1)]
    bufs = [(sa_sc, ba_sc), (sb_sc, bb_sc)]
    win_scores(tiles[0], *bufs[0])
    m_win = m0
    for n, tile in enumerate(tiles):
        if n + 1 < len(tiles):
            win_scores(tiles[n + 1], *bufs[(n + 1) % 2])
        s_buf, b_buf = bufs[n % 2]
        m_win = win_absorb(tile, s_buf, m_win, b_buf)

    gate = gate_ref[0]
    gate_t = gate.T
    ocmp = ocmp_ref[0]
    for hh in range(NSA_GROUP):
        cols = slice(hh * HEAD_DIM, (hh + 1) * HEAD_DIM)
        o_sel_t = acc_sc[:HEAD_DIM, heads[hh]] * (1.0 / acc_sc[HEAD_DIM:HEAD_DIM + 1, heads[hh]])
        o_win_t = accw_sc[:HEAD_DIM, heads[hh]] * (1.0 / accw_sc[HEAD_DIM:HEAD_DIM + 1, heads[hh]])
        z_t = gate_t[3 * hh + 1:3 * hh + 2, :] * o_sel_t + gate_t[3 * hh + 2:3 * hh + 3, :] * o_win_t
        o = gate[:, 3 * hh:3 * hh + 1] * ocmp[:, cols].astype(F32) + z_t.T
        o_ref[0, :, cols] = o.astype(BF16)


def _selwin(q_rot, kv4, vt2, msk, emat, ocmp, gates, b, s):
    q3 = q_rot.reshape(b, s, Q_W)
    g3 = gates.reshape(b, s, NSA_KV_GROUPS * LANES)

    def k_spec(kind):
        return pl.BlockSpec((1, 1, 1, s, HEAD_DIM), lambda bi, g, i: (kind, bi, g, 0, 0))

    def vt_spec(kind):
        return pl.BlockSpec((1, 1, 1, s // TK_SEL, VT_ROWS, TK_SEL), lambda bi, g, i: (kind, bi, g, 0, 0, 0))

    return pl.pallas_call(
        _selwin_kernel,
        grid=(b, NSA_KV_GROUPS, s // TQ),
        in_specs=[
            pl.BlockSpec((1, TQ, GROUP_W), lambda bi, g, i: (bi, i, g)),
            k_spec(ROW_K_SEL), vt_spec(0), k_spec(ROW_K_WIN), vt_spec(1),
            pl.BlockSpec((1, 1, LANES, TQ), lambda bi, g, i: (bi, g, 0, i)),
            pl.BlockSpec((s // TK_SEL, TK_SEL, LANES), lambda bi, g, i: (0, 0, 0)),
            pl.BlockSpec((1, TQ, GROUP_W), lambda bi, g, i: (bi, i, g)),
            pl.BlockSpec((1, TQ, LANES), lambda bi, g, i: (bi, i, g)),
        ],
        out_specs=pl.BlockSpec((1, TQ, GROUP_W), lambda bi, g, i: (bi, i, g)),
        out_shape=jax.ShapeDtypeStruct((b, s, Q_W), BF16),
        scratch_shapes=[
            pltpu.VMEM((VT_ROWS, NSA_GROUP * TQ), F32),
            pltpu.VMEM((VT_ROWS, NSA_GROUP * TQ), F32),
            pltpu.VMEM((TK_SEL, NSA_GROUP * TQ), F32),
            pltpu.VMEM((TK_SEL, NSA_GROUP * TQ), F32),
            pltpu.VMEM((TK_SEL, TQ), F32),
            pltpu.VMEM((TK_SEL, TQ), F32),
        ],
        compiler_params=_params(("parallel", "parallel", "parallel")),
        name="selwin",
    )(q3, kv4, vt2, kv4, vt2, msk, emat, ocmp, g3)


def _softplus(x):
    return jnp.maximum(x, 0.0) + jnp.log1p(jnp.exp(-jnp.abs(x)))


def _lru_kernel(lx_ref, ly_ref, cw_ref, cb_ref, wa_ref, ba_ref, wi_ref, bi_ref, lam_ref, o_ref, xe_sc, h_sc):
    ts = TS_LRU
    halo = SUBLANES

    @pl.when(pl.program_id(2) == 0)
    def _():
        xe_sc[0:halo, :] = jnp.zeros((halo, CW_LRU), F32)
        h_sc[...] = jnp.zeros(h_sc.shape, F32)

    x = lx_ref[0].astype(F32)
    xe_sc[halo:halo + ts, :] = x
    xc = cb_ref[...]
    for k in range(CONV_WIDTH):
        d = CONV_WIDTH - 1 - k
        xc = xc + xe_sc[pl.ds(halo - d, ts), :] * cw_ref[k:k + 1, :]
    xe_sc[0:halo, :] = x[ts - halo:ts]

    row_in_group = lax.broadcasted_iota(I32, (ts, LRU_BLOCK_W), 0) & (SUBLANES - 1)
    for n in range(CW_LRU // LRU_BLOCK_W):
        cols = slice(n * LRU_BLOCK_W, (n + 1) * LRU_BLOCK_W)
        xcn = xc[:, cols]
        xb = xcn.astype(BF16)
        r = jax.nn.sigmoid(_dot(xb, wa_ref[n]) + ba_ref[:, cols])
        ig = jax.nn.sigmoid(_dot(xb, wi_ref[n]) + bi_ref[:, cols])
        log_a = (-LRU_C * _softplus(-lam_ref[:, cols])) * r
        a = jnp.exp(log_a)
        u = jnp.sqrt(-jnp.tanh(log_a) * (a * a + 1.0)) * (ig * xcn)
        ca, cb = a, u
        for d in (1, 2, 4):
            ok = row_in_group >= d
            ca_sh = pltpu.roll(ca, d, axis=0)
            cb_sh = pltpu.roll(cb, d, axis=0)
            cb = jnp.where(ok, ca * cb_sh + cb, cb)
            ca = jnp.where(ok, ca * ca_sh, ca)
        hprev = h_sc[:, cols]
        hs = []
        for gi in range(ts // SUBLANES):
            rows = slice(gi * SUBLANES, (gi + 1) * SUBLANES)
            hg = ca[rows] * hprev + cb[rows]
            hs.append(hg)
            hprev = hg[SUBLANES - 1:SUBLANES]
        h_sc[:, cols] = hprev
        h = jnp.concatenate(hs, axis=0)
        o_ref[0, :, cols] = (h * jax.nn.gelu(ly_ref[0, :, cols].astype(F32))).astype(BF16)


def _lru(lxy, conv_w, conv_b, wa, ba, wi, bi, lam, b, s):
    lxy3 = lxy.reshape(b, s, 2 * LRU_WIDTH)
    ncw = LRU_WIDTH // CW_LRU
    nb = CW_LRU // LRU_BLOCK_W
    vec = pl.BlockSpec((1, CW_LRU), lambda bi_, c, t: (0, c))
    return pl.pallas_call(
        _lru_kernel,
        grid=(b, ncw, s // TS_LRU),
        in_specs=[
            pl.BlockSpec((1, TS_LRU, CW_LRU), lambda bi_, c, t: (bi_, t, c)),
            pl.BlockSpec((1, TS_LRU, CW_LRU), lambda bi_, c, t: (bi_, t, ncw + c)),
            pl.BlockSpec((CONV_WIDTH, CW_LRU), lambda bi_, c, t: (0, c)),
            vec,
            pl.BlockSpec((nb, LRU_BLOCK_W, LRU_BLOCK_W), lambda bi_, c, t: (c, 0, 0)),
            vec,
            pl.BlockSpec((nb, LRU_BLOCK_W, LRU_BLOCK_W), lambda bi_, c, t: (c, 0, 0)),
            vec,
            vec,
        ],
        out_specs=pl.BlockSpec((1, TS_LRU, CW_LRU), lambda bi_, c, t: (bi_, t, c)),
        out_shape=jax.ShapeDtypeStruct((b, s, LRU_WIDTH), BF16),
        scratch_shapes=[
            pltpu.VMEM((TS_LRU + SUBLANES, CW_LRU), F32),
            pltpu.VMEM((1, CW_LRU), F32),
        ],
        compiler_params=_params(("parallel", "parallel", "arbitrary")),
        name="lru",
    )(lxy3, lxy3, conv_w, conv_b, wa, ba, wi, bi, lam)


def _merge_kernel(h_ref, oa_ref, ob_ref, wga_ref, wgb_ref, wpa_ref, wpb_ref, o_ref):
    h = h_ref[...]
    ga = jax.nn.sigmoid(_dot(h, wga_ref[...]))
    gb = jax.nn.sigmoid(_dot(h, wgb_ref[...]))
    o_ref[...] = (ga * _dot(oa_ref[...], wpa_ref[...]) + gb * _dot(ob_ref[...], wpb_ref[...])).astype(BF16)


def _merge(h, oa, ob, wga, wgb, wpa, wpb):
    t = h.shape[0]
    row = pl.BlockSpec((TM, D_MODEL), lambda j, i: (i, 0))
    col = pl.BlockSpec((D_MODEL, TN), lambda j, i: (0, j))
    return pl.pallas_call(
        _merge_kernel,
        grid=(D_MODEL // TN, t // TM),
        in_specs=[row, row, row, col, col, col, col],
        out_specs=pl.BlockSpec((TM, TN), lambda j, i: (i, j)),
        out_shape=jax.ShapeDtypeStruct((t, D_MODEL), BF16),
        compiler_params=_params(("parallel", "parallel"), VMEM_LIMIT),
        name="merge",
    )(h, oa, ob, wga, wgb, wpa, wpb)


def _outrouter_kernel(mg_ref, x_ref, wo_ref, g2_ref, rw_ref, rb_ref, tri_ref,
                      x1_ref, h2_ref, tope_ref, gate_ref, rank_ref, cnt_ref, carry_sc):
    tm = TM_OUT

    @pl.when(pl.program_id(0) == 0)
    def _():
        carry_sc[...] = jnp.zeros(carry_sc.shape, F32)

    x1 = x_ref[...] + _dot(mg_ref[...], wo_ref[...])
    x1_ref[...] = x1
    y = x1 * lax.rsqrt(jnp.mean(x1 * x1, axis=-1, keepdims=True) + NORM_EPS) * g2_ref[...]
    h2_ref[...] = y
    logits = _dot(y.astype(BF16), rw_ref[...]) + rb_ref[...]
    lane = lax.broadcasted_iota(I32, (tm, LANES), 1)
    lane_f = lane.astype(F32)
    work = jnp.where(lane < N_EXPERTS, logits, -jnp.inf)
    vals, idxs = [], []
    for _k in range(TOP_K):
        m = jnp.max(work, axis=-1, keepdims=True)
        idx = jnp.min(jnp.where(work == m, lane_f, float(LANES)), axis=-1, keepdims=True).astype(I32)
        vals.append(m)
        idxs.append(idx)
        work = jnp.where(lane == idx, -jnp.inf, work)
    es = [jnp.exp(v - vals[0]) for v in vals]
    denom = es[0] + es[1] + es[2] + es[3]
    cnt = jnp.zeros((tm, LANES), F32)
    for idx in idxs:
        cnt = cnt + (lane == idx).astype(F32)
    before = _dot(tri_ref[...], cnt.astype(BF16)) + carry_sc[...]
    tope = jnp.zeros((tm, LANES), I32)
    gate = jnp.zeros((tm, LANES), F32)
    rank = jnp.zeros((tm, LANES), F32)
    for k in range(TOP_K):
        tope = jnp.where(lane == k, idxs[k], tope)
        gate = jnp.where(lane == k, es[k] / denom, gate)
        rk = jnp.sum(jnp.where(lane == idxs[k], before, 0.0), axis=-1, keepdims=True)
        rank = jnp.where(lane == k, rk, rank)
    tope_ref[...] = tope
    gate_ref[...] = gate
    rank_ref[...] = rank.astype(I32)
    carry = carry_sc[...] + jnp.sum(cnt, axis=0, keepdims=True)
    carry_sc[...] = carry
    cnt_ref[...] = jnp.broadcast_to(carry, cnt_ref.shape).astype(I32)


def _outrouter(merged, x2, wo, g2, rw, rb, tri):
    t = x2.shape[0]
    row_bf = pl.BlockSpec((TM_OUT, D_MODEL), lambda i: (i, 0))
    lane_blk = pl.BlockSpec((TM_OUT, LANES), lambda i: (i, 0))
    return pl.pallas_call(
        _outrouter_kernel,
        grid=(t // TM_OUT,),
        in_specs=[
            row_bf,
            row_bf,
            pl.BlockSpec((D_MODEL, D_MODEL), lambda i: (0, 0)),
            pl.BlockSpec((1, D_MODEL), lambda i: (0, 0)),
            pl.BlockSpec((D_MODEL, LANES), lambda i: (0, 0)),
            pl.BlockSpec((1, LANES), lambda i: (0, 0)),
            pl.BlockSpec((TM_OUT, TM_OUT), lambda i: (0, 0)),
        ],
        out_specs=[row_bf, row_bf, lane_blk, lane_blk, lane_blk,
                   pl.BlockSpec((SUBLANES, LANES), lambda i: (0, 0))],
        out_shape=[
            jax.ShapeDtypeStruct((t, D_MODEL), F32),
            jax.ShapeDtypeStruct((t, D_MODEL), F32),
            jax.ShapeDtypeStruct((t, LANES), I32),
            jax.ShapeDtypeStruct((t, LANES), F32),
            jax.ShapeDtypeStruct((t, LANES), I32),
            jax.ShapeDtypeStruct((SUBLANES, LANES), I32),
        ],
        scratch_shapes=[pltpu.VMEM((1, LANES), F32)],
        compiler_params=_params(("arbitrary",), VMEM_LIMIT),
        name="outrouter",
    )(merged, x2, wo, g2, rw, rb, tri)


def _row_copy(src_hbm, dst, src_row, dst_row, sem):
    return pltpu.make_async_copy(src_hbm.at[pl.ds(src_row, 1)], dst.at[pl.ds(dst_row, 1)], sem)


def _issue_rows(n_rows, copy_of_row, n_queues=1, first=0):
    for r in range(first, first + n_rows):
        copy_of_row(r).start(priority=r % n_queues)


def _drain_rows(n_rows, any_row_copy):
    def body(r, carry):
        any_row_copy.wait()
        return carry

    lax.fori_loop(0, n_rows, body, 0, unroll=DMA_UNROLL)


def _experts_kernel(ce_ref, na_ref, tokc_ref, tokn_ref, h_hbm, wg_hbm, bg_ref, wu_hbm, bu_ref, wd_hbm, bd_ref,
                    y_ref, xbuf, sem, wg_sc, wu_sc, wd_sc, sg_sc, su_sc, sd_sc, wsem):
    c = pl.program_id(0)
    n_act = na_ref[0]
    slot = c & 1
    last_c = pl.num_programs(0) - 1
    e = ce_ref[c]
    e_next = ce_ref[jnp.minimum(c + 1, last_c)]
    is_first = jnp.logical_or(c == 0, ce_ref[jnp.maximum(c - 1, 0)] != e)
    next_switch = jnp.logical_and(c + 1 < n_act, e_next != e)

    def gather(tok_ref, s_, part=0, n_parts=1):
        n = MOE_ROWS // n_parts
        _issue_rows(n, lambda r: _row_copy(h_hbm, xbuf.at[s_], tok_ref[0, 0, r], r, sem.at[s_]), first=part * n)

    def stage_copies(ex, s, sl):
        span = pl.ds(s * W_STAGE, W_STAGE)
        full = pl.ds(0, D_MODEL)
        return (pltpu.make_async_copy(wg_hbm.at[ex, full, span], sg_sc.at[sl], wsem.at[sl, 0]),
                pltpu.make_async_copy(wu_hbm.at[ex, full, span], su_sc.at[sl], wsem.at[sl, 1]),
                pltpu.make_async_copy(wd_hbm.at[ex, span, full], sd_sc.at[sl], wsem.at[sl, 2]))

    def start_stage(ex, s, sl):
        for cp in stage_copies(ex, s, sl):
            cp.start(priority=1)

    def prefetch_next():
        for s in range(N_WSLOTS):
            start_stage(e_next, s, s)

    def ffn_chunk(x, acc, lo, width):
        cols = slice(lo, lo + width)
        g = _dot(x, wg_sc[:, cols]) + bg_ref[0, :, cols]
        u = _dot(x, wu_sc[:, cols]) + bu_ref[0, :, cols]
        g = jnp.minimum(g, SWIGLU_LIMIT)
        u = jnp.clip(u, -SWIGLU_LIMIT, SWIGLU_LIMIT)
        act = (u + 1.0) * (g * jax.nn.sigmoid(SWIGLU_ALPHA * g))
        return acc + _dot(act.astype(BF16), wd_sc[cols, :])

    @pl.when(jnp.logical_and(c == 0, n_act > 0))
    def _():
        gather(tokc_ref, 0)
        for s in range(N_WSLOTS):
            start_stage(e, s, s)

    @pl.when(jnp.logical_and(c == n_act, n_act > 0))
    def _():
        _drain_rows(MOE_ROWS, _row_copy(h_hbm, xbuf.at[slot], 0, 0, sem.at[slot]))

    @pl.when(c < n_act)
    def _():
        _drain_rows(MOE_ROWS, _row_copy(h_hbm, xbuf.at[slot], 0, 0, sem.at[slot]))
        x = xbuf[slot].astype(BF16)

        @pl.when(is_first)
        def _():
            n_stages = D_FF // W_STAGE
            acc = jnp.zeros((MOE_ROWS, D_MODEL), F32)
            for s in range(n_stages):
                sl = s % N_WSLOTS
                span = slice(s * W_STAGE, (s + 1) * W_STAGE)
                for cp in stage_copies(e, s, sl):
                    cp.wait()
                wg_sc[:, span] = sg_sc[sl].astype(BF16)
                wu_sc[:, span] = su_sc[sl].astype(BF16)
                wd_sc[span, :] = sd_sc[sl].astype(BF16)
                if s + N_WSLOTS < n_stages:
                    start_stage(e, s + N_WSLOTS, sl)
                gather(tokn_ref, 1 - slot, s, n_stages)
                acc = ffn_chunk(x, acc, s * W_STAGE, W_STAGE)
            y_ref[...] = acc + bd_ref[0]

            @pl.when(next_switch)
            def _():
                prefetch_next()

        @pl.when(jnp.logical_not(is_first))
        def _():
            @pl.when(next_switch)
            def _():
                prefetch_next()

            n_parts = D_FF // FF_CHUNK
            acc = jnp.zeros((MOE_ROWS, D_MODEL), F32)
            for f in range(n_parts):
                gather(tokn_ref, 1 - slot, f, n_parts)
                acc = ffn_chunk(x, acc, f * FF_CHUNK, FF_CHUNK)
            y_ref[...] = acc + bd_ref[0]

    @pl.when(c >= n_act)
    def _():
        y_ref[...] = jnp.zeros(y_ref.shape, y_ref.dtype)


def _experts(chunk_e, n_active, row_tok, h2, wg, bg, wu, bu, wd, bd, n_chunks):
    def wspec():
        return pl.BlockSpec(memory_space=pl.ANY)

    def bspec():
        return pl.BlockSpec((1, 1, D_FF), lambda c, ce, na: (ce[c], 0, 0))

    tok3 = row_tok.reshape(n_chunks, 1, MOE_ROWS)
    grid_spec = pltpu.PrefetchScalarGridSpec(
        num_scalar_prefetch=2,
        grid=(n_chunks,),
        in_specs=[
            pl.BlockSpec((1, 1, MOE_ROWS), lambda c, ce, na: (c, 0, 0), memory_space=pltpu.SMEM),
            pl.BlockSpec((1, 1, MOE_ROWS), lambda c, ce, na: (jnp.minimum(c + 1, n_chunks - 1), 0, 0),
                         memory_space=pltpu.SMEM),
            pl.BlockSpec(memory_space=pl.ANY),
            wspec(), bspec(),
            wspec(), bspec(),
            wspec(), bspec(),
        ],
        out_specs=pl.BlockSpec((MOE_ROWS, D_MODEL), lambda c, ce, na: (c, 0)),
        scratch_shapes=[
            pltpu.VMEM((2, MOE_ROWS, D_MODEL), F32),
            pltpu.SemaphoreType.DMA((2,)),
            pltpu.VMEM((D_MODEL, D_FF), BF16),
            pltpu.VMEM((D_MODEL, D_FF), BF16),
            pltpu.VMEM((D_FF, D_MODEL), BF16),
            pltpu.VMEM((N_WSLOTS, D_MODEL, W_STAGE), F32),
            pltpu.VMEM((N_WSLOTS, D_MODEL, W_STAGE), F32),
            pltpu.VMEM((N_WSLOTS, W_STAGE, D_MODEL), F32),
            pltpu.SemaphoreType.DMA((N_WSLOTS, 3)),
        ],
    )
    return pl.pallas_call(
        _experts_kernel,
        grid_spec=grid_spec,
        out_shape=jax.ShapeDtypeStruct((n_chunks * MOE_ROWS, D_MODEL), F32),
        compiler_params=_params(("arbitrary",), VMEM_LIMIT_EXPERTS),
        name="experts",
    )(chunk_e, n_active, tok3, tok3, h2, wg, bg, wu, bu, wd, bd)


def _combine_kernel(destc_ref, destn_ref, gate_ref, x1_ref, gf_ref, y_hbm, o_ref, ybuf, sem):
    i = pl.program_id(0)
    slot = i & 1
    n = TB_COMBINE * TOP_K

    def gather(dest_ref, s_):
        _issue_rows(n, lambda r: _row_copy(y_hbm, ybuf.at[s_, r & (TOP_K - 1)], dest_ref[0, 0, r],
                                           r >> TOP_K_LOG2, sem.at[s_]), n_queues=2)

    @pl.when(i == 0)
    def _():
        gather(destc_ref, 0)

    @pl.when(i + 1 < pl.num_programs(0))
    def _():
        gather(destn_ref, 1 - slot)

    _drain_rows(n, _row_copy(y_hbm, ybuf.at[slot, 0], 0, 0, sem.at[slot]))
    gate = gate_ref[...]
    out = x1_ref[...]
    for k in range(TOP_K):
        out = out + gate[:, k:k + 1] * ybuf[slot, k]
    o_ref[...] = out * lax.rsqrt(jnp.mean(out * out, axis=-1, keepdims=True) + NORM_EPS) * gf_ref[...]


def _combine(dest, gate, x1, gf, y_rows):
    t = x1.shape[0]
    nt = t // TB_COMBINE
    dest3 = dest.reshape(nt, 1, TB_COMBINE * TOP_K)
    return pl.pallas_call(
        _combine_kernel,
        grid=(nt,),
        in_specs=[
            pl.BlockSpec((1, 1, TB_COMBINE * TOP_K), lambda i: (i, 0, 0), memory_space=pltpu.SMEM),
            pl.BlockSpec((1, 1, TB_COMBINE * TOP_K), lambda i: (jnp.minimum(i + 1, nt - 1), 0, 0),
                         memory_space=pltpu.SMEM),
            pl.BlockSpec((TB_COMBINE, LANES), lambda i: (i, 0)),
            pl.BlockSpec((TB_COMBINE, D_MODEL), lambda i: (i, 0)),
            pl.BlockSpec((1, D_MODEL), lambda i: (0, 0)),
            pl.BlockSpec(memory_space=pl.ANY),
        ],
        out_specs=pl.BlockSpec((TB_COMBINE, D_MODEL), lambda i: (i, 0)),
        out_shape=jax.ShapeDtypeStruct((t, D_MODEL), F32),
        scratch_shapes=[
            pltpu.VMEM((2, TOP_K, TB_COMBINE, D_MODEL), F32),
            pltpu.SemaphoreType.DMA((2,)),
        ],
        compiler_params=_params(("arbitrary",)),
        name="combine",
    )(dest3, dest3, gate, x1, gf, y_rows)


def _rope_tables(s):
    pos = jnp.arange(s, dtype=F32)
    inv = ROPE_THETA ** (-jnp.arange(0, HEAD_DIM, 2, dtype=F32) / HEAD_DIM)
    ang = pos[:, None] * inv[None, :]
    ang = jnp.concatenate([ang, ang], axis=-1)
    sign = jnp.where(jnp.arange(HEAD_DIM) < HEAD_DIM // 2, -1.0, 1.0).astype(F32)
    return jnp.cos(ang), jnp.sin(ang) * sign[None, :]


def _cmp_to_sel_t(nc, n_sel):
    cs = np.arange(nc) * CMP_STRIDE
    ss = np.arange(n_sel) * SEL_BLOCK
    ov = np.minimum(cs[:, None] + CMP_BLOCK, ss[None, :] + SEL_BLOCK) - np.maximum(cs[:, None], ss[None, :])
    m = np.clip(ov, 0, None) / CMP_BLOCK
    m[nc - 1, :] = 0.0
    out = np.zeros((LANES, nc), np.float32)
    out[:n_sel] = m.T
    return jnp.asarray(out, dtype=BF16)


def _block_expand(s):
    key = np.arange(s).reshape(s // TK_SEL, TK_SEL, 1)
    blk = np.arange(LANES).reshape(1, 1, LANES)
    return jnp.asarray((key // SEL_BLOCK == blk).astype(np.float32), dtype=BF16)


def kernel(x, norm1_g, w_in, cmp_pe_k, cmp_pe_v, cmp_k_w1, cmp_k_w2, cmp_v_w1, cmp_v_w2, lru_conv_w, lru_conv_b, lru_wa, lru_ba, lru_wi, lru_bi, lru_lambda, w_br_attn, w_br_lru, w_out, norm2_g, router_w, router_b, exp_w_gate, exp_b_gate, exp_w_up, exp_b_up, exp_w_down, exp_b_down, normf_g):
    b, s, d = x.shape
    assert d == D_MODEL and norm1_g.shape[0] == 1, "single-layer block of width D_MODEL"
    assert s % TM == 0 and s % TS_LRU == 0 and s >= WIN_KEYS and s // SEL_BLOCK <= LANES
    assert s % (2 * TK_SEL) == 0, "the selected branch walks key tiles in pairs"
    t = b * s
    x2 = x.reshape(t, d)

    w_in0 = w_in[0]
    o = 0
    wq = w_in0[:, o:o + Q_W].astype(BF16); o += Q_W
    wkv = w_in0[:, o:o + KV_KINDS * KV_W].reshape(d, KV_KINDS, KV_W); o += KV_KINDS * KV_W
    wk4 = wkv[:, (0, 1, 2, 4), :].reshape(d, ROW_KINDS * KV_W).astype(BF16)
    wv2 = wkv[:, (3, 5), :].reshape(d, 2 * KV_W).astype(BF16)
    n_g = 3 * NSA_Q_HEADS
    wg_nsa = w_in0[:, o:o + n_g].reshape(d, NSA_KV_GROUPS, 3 * NSA_GROUP); o += n_g
    wg_nsa = jnp.pad(wg_nsa, ((0, 0), (0, 0), (0, LANES - 3 * NSA_GROUP))).reshape(d, NSA_KV_GROUPS * LANES).astype(BF16)
    wlxy = w_in0[:, o:o + 2 * LRU_WIDTH].astype(BF16); o += 2 * LRU_WIDTH
    wga = w_in0[:, o:o + D_MODEL].astype(BF16); o += D_MODEL
    wgb = w_in0[:, o:o + D_MODEL].astype(BF16); o += D_MODEL

    cos, sin_signed = _rope_tables(s)
    nc = s // CMP_STRIDE
    n_sel = s // SEL_BLOCK

    h, gates = _norm1(x2, norm1_g, wg_nsa)
    q_raw, q_rot = _qproj(h, wq, cos, sin_signed, s)
    kv4 = _kvproj(h, wk4, cos, sin_signed, b, s)
    vt2 = _vtproj(h, wv2, b, s)
    lxy = _proj(h, wlxy, "lruproj")

    cw1 = jnp.stack([cmp_k_w1[0], cmp_v_w1[0]]).astype(BF16)
    cw2 = jnp.stack([cmp_k_w2[0], cmp_v_w2[0]]).astype(BF16)
    pe = jnp.stack([cmp_pe_k[0], cmp_pe_v[0]]).reshape(2, 1, CMP_BLOCK * HEAD_DIM)
    pe = jnp.pad(pe, ((0, 0), (0, SUBLANES - 1), (0, 0))).astype(BF16)
    kvc = _cmpkv(kv4, cw1, cw2, pe, b, s)
    ocmp, msk = _cmpattn(q_raw, kvc, _cmp_to_sel_t(nc, n_sel), b, s)
    o_a = _selwin(q_rot, kv4, vt2, msk, _block_expand(s), ocmp, gates, b, s)

    o_b = _lru(lxy, lru_conv_w[0], lru_conv_b, lru_wa[0].astype(BF16), lru_ba[0].reshape(1, LRU_WIDTH),
               lru_wi[0].astype(BF16), lru_bi[0].reshape(1, LRU_WIDTH), lru_lambda, b, s)

    merged = _merge(h, o_a.reshape(t, Q_W), o_b.reshape(t, LRU_WIDTH), wga, wgb,
                    w_br_attn[0].astype(BF16), w_br_lru[0].astype(BF16))

    rw = jnp.pad(router_w[0], ((0, 0), (0, LANES - N_EXPERTS))).astype(BF16)
    rb = jnp.pad(router_b, ((0, 0), (0, LANES - N_EXPERTS)))
    tri = jnp.asarray(np.tril(np.ones((TM_OUT, TM_OUT), np.float32), -1), dtype=BF16)
    x1, h2, tope, gate, rank, cnt = _outrouter(merged, x2, w_out[0].astype(BF16), norm2_g, rw, rb, tri)

    counts = cnt[0, :N_EXPERTS]
    padded = (counts + MOE_ROWS - 1) // MOE_ROWS * MOE_ROWS
    pend = jnp.cumsum(padded)
    pstart = pend - padded
    n_rows = t * TOP_K + N_EXPERTS * MOE_ROWS
    n_chunks = n_rows // MOE_ROWS
    dest = (pstart[tope[:, :TOP_K]] + rank[:, :TOP_K]).astype(I32)
    tok_of = jnp.broadcast_to(jnp.arange(t, dtype=I32)[:, None], (t, TOP_K))
    row_tok = jnp.zeros((n_rows,), I32).at[dest.reshape(-1)].set(tok_of.reshape(-1))
    chunk_e = jnp.minimum(jnp.sum((jnp.arange(n_chunks) * MOE_ROWS)[:, None] >= pend[None, :], axis=1),
                          N_EXPERTS - 1).astype(I32)
    n_active = (pend[-1:] // MOE_ROWS).astype(I32)

    y_rows = _experts(chunk_e, n_active, row_tok, h2,
                      exp_w_gate.reshape(N_EXPERTS, D_MODEL, D_FF), exp_b_gate[0].reshape(N_EXPERTS, 1, D_FF),
                      exp_w_up.reshape(N_EXPERTS, D_MODEL, D_FF), exp_b_up[0].reshape(N_EXPERTS, 1, D_FF),
                      exp_w_down.reshape(N_EXPERTS, D_FF, D_MODEL), exp_b_down[0].reshape(N_EXPERTS, 1, D_MODEL),
                      n_chunks)
    out = _combine(dest, gate, x1, normf_g.reshape(1, D_MODEL), y_rows)
    return out.reshape(b, s, d)
```
